```python
import jax, jax.numpy as jnp
from jax import lax
import numpy as np


D_MODEL = 2048
BATCH = 2
SEQ = 8192
DEPTH = 2
DEC_BATCH = 16
DEC_SEQ = 2048
PAST_LEN = 128

HEAD_DIM = 128
N_HEADS = D_MODEL // HEAD_DIM
DA_HEADS = N_HEADS // 2
DA_QK_DIM = HEAD_DIM // 2
DA_V_DIM = HEAD_DIM
RET_HEADS = N_HEADS - DA_HEADS
RET_QK_DIM = HEAD_DIM
RET_V_DIM = HEAD_DIM
RET_CHUNK = 128
Q_BLOCK = 128
ROPE_THETA = 10000.0
A_QK = DA_HEADS * 2 * DA_QK_DIM
A_V = DA_HEADS * DA_V_DIM
B_QK = RET_HEADS * RET_QK_DIM
B_V = RET_HEADS * RET_V_DIM
AB_IN = 2 * A_QK + A_V + 2 * B_QK + 2 * B_V
AB_OUT = A_V + B_V
D_RNN = ((4 * D_MODEL // 3 + 127) // 128) * 128
RG_BLOCKS = 16
RG_BS = D_RNN // RG_BLOCKS
RG_C = 8.0
CONV_WIDTH = 4
N_EXPERTS = 16
CAPACITY_FACTOR = 2
D_FF_EXPERT = ((8 * D_MODEL // 3 + 255) // 256) * 256
N_EVEN = (DEPTH + 1) // 2
N_ODD = DEPTH // 2
EPS = 1e-6

kernel_name = 'hybrid_diffattn_retention_rglru_ec_moe_encoder'


def rms_norm(x, g):
    xf = x.astype(jnp.float32)
    y = xf * lax.rsqrt(jnp.mean(xf * xf, axis=-1, keepdims=True) + EPS)
    return (y * g.astype(jnp.float32)).astype(x.dtype)


def rope(x):
    S, d = x.shape[1], x.shape[-1]
    half = d // 2
    inv = ROPE_THETA ** (-jnp.arange(half, dtype=jnp.float32) / half)
    ang = jnp.arange(S, dtype=jnp.float32)[:, None] * inv[None, :]
    shape = (S,) + (1,) * (x.ndim - 3) + (half,)
    cos = jnp.cos(ang).reshape(shape).astype(x.dtype)
    sin = jnp.sin(ang).reshape(shape).astype(x.dtype)
    x1, x2 = x[..., :half], x[..., half:]
    return jnp.concatenate([x1 * cos - x2 * sin, x2 * cos + x1 * sin], axis=-1)


def diff_attention(q1, q2, k1, k2, v, lam):
    B, S, H, d = q1.shape
    nb = S // Q_BLOCK
    scale = d ** -0.5
    qs = jnp.stack([q1, q2], 0).reshape(2, B, nb, Q_BLOCK, H, d).transpose(2, 0, 1, 3, 4, 5)
    ks = jnp.stack([k1, k2], 0)

    def block(qb):
        s = jnp.einsum('gbqhd,gbkhd->gbhqk', qb, ks).astype(jnp.float32) * scale
        p = jax.nn.softmax(s, axis=-1)
        a = (p[0] - lam * p[1]).astype(v.dtype)
        return jnp.einsum('bhqk,bkhe->bqhe', a, v)

    out = lax.map(block, qs)
    return out.transpose(1, 0, 2, 3, 4).reshape(B, S, H, v.shape[-1])


def retention_one_dir(q, k, v, strict):
    B, S, H, d = q.shape
    e = v.shape[-1]
    C = RET_CHUNK
    nc = S // C
    log_g = jnp.log1p(-(2.0 ** (-5.0 - jnp.arange(H, dtype=jnp.float32))))
    pos = jnp.arange(C, dtype=jnp.float32)
    diff = pos[:, None] - pos[None, :]
    mask = (diff > 0) if strict else (diff >= 0)
    decay = jnp.where(mask[None], jnp.exp(jnp.where(mask, diff, 0.0)[None] * log_g[:, None, None]), 0.0)
    qc = q.reshape(B, nc, C, H, d)
    kc = k.reshape(B, nc, C, H, d)
    vc = v.reshape(B, nc, C, H, e)
    scores = jnp.einsum('bnqhd,bnkhd->bnhqk', qc, kc) * decay
    inner = jnp.einsum('bnhqk,bnkhe->bnqhe', scores, vc)
    k_dec = jnp.exp((C - 1 - pos)[:, None] * log_g[None, :])
    kv = jnp.einsum('bnkhd,kh,bnkhe->nbhde', kc, k_dec, vc)
    chunk_dec = jnp.exp(C * log_g)[:, None, None]

    def step(R, kv_i):
        return R * chunk_dec + kv_i, R

    _, r_prev = lax.scan(step, jnp.zeros((B, H, d, e), jnp.float32), kv)
    q_dec = jnp.exp((pos + 1.0)[:, None] * log_g[None, :])
    cross = jnp.einsum('bnqhd,qh,nbhde->bnqhe', qc, q_dec, r_prev)
    return (inner + cross).reshape(B, S, H, e)


def bidir_retention(q, k, v):
    q, k, v = q.astype(jnp.float32), k.astype(jnp.float32), v.astype(jnp.float32)
    fwd = retention_one_dir(q, k, v, False)
    bwd = retention_one_dir(jnp.flip(q, 1), jnp.flip(k, 1), jnp.flip(v, 1), True)
    return fwd + jnp.flip(bwd, 1)


def mixer_ab(h, w_in, w_out, q_norm, k_norm, lq1, lk1, lq2, lk2, subln, ret_norm, layer_idx):
    B, S, _ = h.shape
    proj = h @ w_in
    offs = np.cumsum([A_QK, A_QK, A_V, B_QK, B_QK, B_V]).tolist()
    qa, ka, va, qb, kb, vb, gb = jnp.split(proj, offs, axis=-1)
    qa = rope(rms_norm(qa.reshape(B, S, DA_HEADS, 2, DA_QK_DIM), q_norm))
    ka = rope(rms_norm(ka.reshape(B, S, DA_HEADS, 2, DA_QK_DIM), k_norm))
    va = va.reshape(B, S, DA_HEADS, DA_V_DIM)
    lam_init = 0.8 - 0.6 * float(np.exp(-0.3 * layer_idx))
    f32 = jnp.float32
    lam = (jnp.exp(jnp.sum(lq1.astype(f32) * lk1.astype(f32)))
           - jnp.exp(jnp.sum(lq2.astype(f32) * lk2.astype(f32))) + lam_init)
    out_a = diff_attention(qa[..., 0, :], qa[..., 1, :], ka[..., 0, :], ka[..., 1, :], va, lam)
    out_a = (rms_norm(out_a, subln.reshape(DA_HEADS, DA_V_DIM)) * (1.0 - lam_init)).reshape(B, S, A_V)
    qb = rope(qb.reshape(B, S, RET_HEADS, RET_QK_DIM))
    kb = rope(kb.reshape(B, S, RET_HEADS, RET_QK_DIM)) * (RET_QK_DIM ** -0.5)
    vb = vb.reshape(B, S, RET_HEADS, RET_V_DIM)
    ret = bidir_retention(qb, kb, vb).astype(h.dtype)
    ret = rms_norm(ret, ret_norm.reshape(RET_HEADS, RET_V_DIM)).reshape(B, S, B_V)
    out_b = ret * jax.nn.silu(gb)
    return jnp.concatenate([out_a, out_b], axis=-1) @ w_out


def _lru_combine(c1, c2):
    a1, b1 = c1
    a2, b2 = c2
    return a1 * a2, a2 * b1 + b2


def rg_lru(x, wa, ba, wi, bi, lam, reverse):
    B, S, C = x.shape
    xb = x.reshape(B, S, RG_BLOCKS, RG_BS)
    r = jax.nn.sigmoid((jnp.einsum('bsnc,ncd->bsnd', xb, wa).reshape(B, S, C) + ba).astype(jnp.float32))
    i = jax.nn.sigmoid((jnp.einsum('bsnc,ncd->bsnd', xb, wi).reshape(B, S, C) + bi).astype(jnp.float32))
    log_a = -RG_C * jax.nn.softplus(-lam.astype(jnp.float32)) * r
    a = jnp.exp(log_a)
    b = jnp.sqrt(-jnp.expm1(2.0 * log_a)) * (i * x.astype(jnp.float32))
    _, hh = lax.associative_scan(_lru_combine, (a, b), axis=1, reverse=reverse)
    return hh


def mixer_c(h, w_in, conv_w, conv_b, wa, ba, wi, bi, lam, w_out):
    proj = h @ w_in
    gate, xr = jnp.split(proj, 2, axis=-1)
    xr = lax.conv_general_dilated(xr, conv_w[:, None, :], window_strides=(1,), padding=[(2, 1)],
                                  dimension_numbers=('NWC', 'WIO', 'NWC'),
                                  feature_group_count=D_RNN) + conv_b
    hsum = rg_lru(xr, wa[0], ba[0], wi[0], bi[0], lam[0], False) + rg_lru(xr, wa[1], ba[1], wi[1], bi[1], lam[1], True)
    y = jax.nn.gelu(gate) * hsum.astype(h.dtype)
    return y @ w_out


def ec_moe(h, router, wg, wu, wd):
    B, S, D = h.shape
    N = B * S
    cap = max(1, CAPACITY_FACTOR * N // N_EXPERTS)
    xf = h.reshape(N, D)
    aff = jax.nn.softmax((xf @ router).astype(jnp.float32), axis=-1)
    gates, idx = lax.top_k(aff.T, cap)

    def expert(args):
        idx_e, g_e, wg_e, wu_e, wd_e = args
        xe = xf[idx_e]
        hdn = jax.nn.silu(xe @ wg_e) * (xe @ wu_e)
        return (hdn @ wd_e) * g_e[:, None].astype(xe.dtype)

    outs = lax.map(expert, (idx, gates, wg, wu, wd))
    y = jnp.zeros_like(xf).at[idx.reshape(-1)].add(outs.reshape(-1, D))
    return y.reshape(B, S, D)


def trunk(x, norm_mix, norm_ffn, att_w_in, att_w_out, att_q_norm, att_k_norm, att_lam_q1, att_lam_k1,
          att_lam_q2, att_lam_k2, att_subln, ret_norm, rg_w_in, rg_conv_w, rg_conv_b, rg_wa, rg_ba,
          rg_wi, rg_bi, rg_lambda, rg_w_out, moe_router, moe_w_gate, moe_w_up, moe_w_down):
    for layer in range(DEPTH):
        h = rms_norm(x, norm_mix[layer])
        j = layer // 2
        if layer % 2 == 0:
            mix = mixer_ab(h, att_w_in[j], att_w_out[j], att_q_norm[j], att_k_norm[j], att_lam_q1[j],
                           att_lam_k1[j], att_lam_q2[j], att_lam_k2[j], att_subln[j], ret_norm[j], layer)
        else:
            mix = mixer_c(h, rg_w_in[j], rg_conv_w[j], rg_conv_b[j], rg_wa[j], rg_ba[j], rg_wi[j],
                          rg_bi[j], rg_lambda[j], rg_w_out[j])
        x = x + mix
        x = x + ec_moe(rms_norm(x, norm_ffn[layer]), moe_router[layer], moe_w_gate[layer],
                       moe_w_up[layer], moe_w_down[layer])
    return x


def setup_inputs(seed: int = 0) -> dict:
    key = jax.random.key(seed)
    ks = jax.random.split(key, 32)
    f32 = jnp.float32

    def nrm(k, shape, scale):
        return jax.random.normal(k, shape, f32) * scale

    u = jax.random.uniform(ks[20], (N_ODD, 2, D_RNN), f32, minval=0.9, maxval=0.999)
    a0 = u ** (1.0 / RG_C)
    rg_lambda = jnp.log(a0) - jnp.log1p(-a0)
    return {
        'x_prompt': nrm(ks[0], (BATCH, SEQ, D_MODEL), 1.0),
        'x_sample': nrm(ks[1], (DEC_BATCH, DEC_SEQ, D_MODEL), 1.0),
        'norm_mix': 1.0 + nrm(ks[2], (DEPTH, D_MODEL), 0.01),
        'norm_ffn': 1.0 + nrm(ks[3], (DEPTH, D_MODEL), 0.01),
        'att_w_in': nrm(ks[4], (N_EVEN, D_MODEL, AB_IN), D_MODEL ** -0.5),
        'att_w_out': nrm(ks[5], (N_EVEN, AB_OUT, D_MODEL), AB_OUT ** -0.5),
        'att_q_norm': 1.0 + nrm(ks[6], (N_EVEN, DA_QK_DIM), 0.01),
        'att_k_norm': 1.0 + nrm(ks[7], (N_EVEN, DA_QK_DIM), 0.01),
        'att_lam_q1': nrm(ks[8], (N_EVEN, DA_QK_DIM), 0.1),
        'att_lam_k1': nrm(ks[9], (N_EVEN, DA_QK_DIM), 0.1),
        'att_lam_q2': nrm(ks[10], (N_EVEN, DA_QK_DIM), 0.1),
        'att_lam_k2': nrm(ks[11], (N_EVEN, DA_QK_DIM), 0.1),
        'att_subln': 1.0 + nrm(ks[12], (N_EVEN, A_V), 0.01),
        'ret_norm': 1.0 + nrm(ks[13], (N_EVEN, B_V), 0.01),
        'rg_w_in': nrm(ks[14], (N_ODD, D_MODEL, 2 * D_RNN), D_MODEL ** -0.5),
        'rg_conv_w': nrm(ks[15], (N_ODD, CONV_WIDTH, D_RNN), CONV_WIDTH ** -0.5),
        'rg_conv_b': nrm(ks[16], (N_ODD, D_RNN), 0.01),
        'rg_wa': nrm(ks[17], (N_ODD, 2, RG_BLOCKS, RG_BS, RG_BS), RG_BS ** -0.5),
        'rg_ba': nrm(ks[18], (N_ODD, 2, D_RNN), 0.01),
        'rg_wi': nrm(ks[19], (N_ODD, 2, RG_BLOCKS, RG_BS, RG_BS), RG_BS ** -0.5),
        'rg_bi': nrm(ks[21], (N_ODD, 2, D_RNN), 0.01),
        'rg_lambda': rg_lambda,
        'rg_w_out': nrm(ks[22], (N_ODD, D_RNN, D_MODEL), D_RNN ** -0.5),
        'moe_router': nrm(ks[23], (DEPTH, D_MODEL, N_EXPERTS), D_MODEL ** -0.5),
        'moe_w_gate': nrm(ks[24], (DEPTH, N_EXPERTS, D_MODEL, D_FF_EXPERT), D_MODEL ** -0.5),
        'moe_w_up': nrm(ks[25], (DEPTH, N_EXPERTS, D_MODEL, D_FF_EXPERT), D_MODEL ** -0.5),
        'moe_w_down': nrm(ks[26], (DEPTH, N_EXPERTS, D_FF_EXPERT, D_MODEL), D_FF_EXPERT ** -0.5),
    }


def reference(x_prompt, x_sample, norm_mix, norm_ffn, att_w_in, att_w_out, att_q_norm, att_k_norm,
              att_lam_q1, att_lam_k1, att_lam_q2, att_lam_k2, att_subln, ret_norm, rg_w_in, rg_conv_w,
              rg_conv_b, rg_wa, rg_ba, rg_wi, rg_bi, rg_lambda, rg_w_out, moe_router, moe_w_gate,
              moe_w_up, moe_w_down):
    y_prompt = trunk(x_prompt, norm_mix, norm_ffn, att_w_in, att_w_out, att_q_norm, att_k_norm,
                     att_lam_q1, att_lam_k1, att_lam_q2, att_lam_k2, att_subln, ret_norm, rg_w_in,
                     rg_conv_w, rg_conv_b, rg_wa, rg_ba, rg_wi, rg_bi, rg_lambda, rg_w_out,
                     moe_router, moe_w_gate, moe_w_up, moe_w_down)
    y_sample = trunk(x_sample, norm_mix, norm_ffn, att_w_in, att_w_out, att_q_norm, att_k_norm,
                     att_lam_q1, att_lam_k1, att_lam_q2, att_lam_k2, att_subln, ret_norm, rg_w_in,
                     rg_conv_w, rg_conv_b, rg_wa, rg_ba, rg_wi, rg_bi, rg_lambda, rg_w_out,
                     moe_router, moe_w_gate, moe_w_up, moe_w_down)
    return (y_prompt, y_sample)
```

```python
import functools
import math

import numpy as np
import jax
import jax.numpy as jnp
from jax import lax
from jax.experimental import pallas as pl
from jax.experimental.pallas import tpu as pltpu

F32 = jnp.float32
BF16 = jnp.bfloat16

HEAD_DIM = 128
DA_QK_DIM = HEAD_DIM // 2
EPS = 1e-6
ROPE_THETA = 10000.0
RG_C = 8.0
CONV_WIDTH = 4
RET_CHUNK = 256
LANES = 128
SUBLANES = 8
MXU_DIM = 256
VMEM_LIMIT = 52 * 1024 * 1024


def _cparams(sem):
    return pltpu.CompilerParams(dimension_semantics=sem, vmem_limit_bytes=VMEM_LIMIT)


def _pick(n, pref):
    t = min(pref, n)
    while n % t:
        t //= 2
    return t


def _norm_matmul_kernel(x_ref, g_ref, w_ref, o_ref, xn_ref):
    @pl.when(pl.program_id(1) == 0)
    def _():
        x = x_ref[...]
        ms = jnp.mean(x * x, axis=-1, keepdims=True)
        xn_ref[...] = (x * lax.rsqrt(ms + EPS) * g_ref[...]).astype(BF16)

    o_ref[...] = jnp.dot(xn_ref[...], w_ref[...], preferred_element_type=F32).astype(o_ref.dtype)


def norm_matmul(x, g, w, out_dtype=F32, tm=512, tn=1024):
    n, d = x.shape
    nout = w.shape[1]
    tm = _pick(n, tm)
    tn = _pick(nout, tn)
    return pl.pallas_call(
        _norm_matmul_kernel,
        grid=(n // tm, nout // tn),
        in_specs=[
            pl.BlockSpec((tm, d), lambda i, j: (i, 0)),
            pl.BlockSpec((1, d), lambda i, j: (0, 0)),
            pl.BlockSpec((d, tn), lambda i, j: (0, j)),
        ],
        out_specs=pl.BlockSpec((tm, tn), lambda i, j: (i, j)),
        out_shape=jax.ShapeDtypeStruct((n, nout), out_dtype),
        scratch_shapes=[pltpu.VMEM((tm, d), BF16)],
        compiler_params=_cparams(("parallel", "arbitrary")),
        name="norm_matmul",
    )(x, g.reshape(1, d), w)


def _matmul_res_kernel(*refs, npairs):
    r_ref, o_ref = refs[2 * npairs], refs[2 * npairs + 1]
    acc = r_ref[...]
    for p in range(npairs):
        acc = acc + jnp.dot(refs[2 * p][...], refs[2 * p + 1][...], preferred_element_type=F32)
    o_ref[...] = acc


def matmul_residual(pairs, res, tm=512, tn=512):
    n, dout = res.shape
    tm = _pick(n, tm)
    tn = _pick(dout, tn)
    in_specs, args = [], []
    for y, w in pairs:
        k = y.shape[1]
        in_specs += [pl.BlockSpec((tm, k), lambda i, j: (i, 0)), pl.BlockSpec((k, tn), lambda i, j: (0, j))]
        args += [y, w]
    in_specs.append(pl.BlockSpec((tm, tn), lambda i, j: (i, j)))
    args.append(res)
    return pl.pallas_call(
        functools.partial(_matmul_res_kernel, npairs=len(pairs)),
        grid=(n // tm, dout // tn),
        in_specs=in_specs,
        out_specs=pl.BlockSpec((tm, tn), lambda i, j: (i, j)),
        out_shape=jax.ShapeDtypeStruct((n, dout), F32),
        compiler_params=_cparams(("parallel", "parallel")),
        name="matmul_residual",
    )(*args)


def _rope_tables(seq, group):
    half = group // 2
    inv = ROPE_THETA ** (-jnp.arange(half, dtype=F32) / half)
    ang = jnp.arange(seq, dtype=F32)[:, None] * inv[None, :]
    cos, sin = jnp.cos(ang), jnp.sin(ang)
    reps = LANES // group
    cos_t = jnp.tile(jnp.concatenate([cos, cos], axis=-1), (1, reps))
    sin_t = jnp.tile(jnp.concatenate([-sin, sin], axis=-1), (1, reps))
    return cos_t, sin_t


def _qk_prep_kernel(x_ref, cos_ref, sin_ref, g_ref, gmat_ref, o_ref, *, group, normed, scales):
    j = pl.program_id(2)
    scale = jnp.where(j == 0, scales[0], scales[1]).astype(F32)
    cos = cos_ref[...]
    sin = sin_ref[...]
    g = g_ref[0]
    nh = x_ref.shape[2] // LANES
    lane = lax.broadcasted_iota(jnp.int32, cos.shape, 1)
    first_half = (lane % group) < (group // 2)
    for h in range(nh):
        x = x_ref[0, :, h * LANES:(h + 1) * LANES]
        if normed:
            sq = x * x
            hi = sq.astype(BF16)
            lo = (sq - hi.astype(F32)).astype(BF16)
            ms = (jnp.dot(hi, gmat_ref[...], preferred_element_type=F32)
                  + jnp.dot(lo, gmat_ref[...], preferred_element_type=F32))
            x = x * lax.rsqrt(ms + EPS) * g
        if group == LANES:
            swapped = pltpu.roll(x, LANES // 2, axis=1)
        else:
            fwd = pltpu.roll(x, group // 2, axis=1)
            bwd = pltpu.roll(x, LANES - group // 2, axis=1)
            swapped = jnp.where(first_half, bwd, fwd)
        y = (x * cos + swapped * sin) * scale
        o_ref[0, :, h * LANES:(h + 1) * LANES] = y.astype(o_ref.dtype)


def qk_prep(proj, col0, width, group, norm_w, scales, ts=512):
    b, s, _ = proj.shape
    ts = _pick(s, ts)
    cos_t, sin_t = _rope_tables(s, group)
    normed = norm_w is not None
    if normed:
        g = jnp.stack([jnp.tile(w.astype(F32), LANES // group) for w in norm_w]).reshape(2, 1, LANES)
    else:
        g = jnp.ones((2, 1, LANES), F32)
    grp = np.arange(LANES) // group
    gmat = jnp.asarray((grp[:, None] == grp[None, :]).astype(np.float32) / group, BF16)
    cb = col0 // width
    return pl.pallas_call(
        functools.partial(_qk_prep_kernel, group=group, normed=normed, scales=scales),
        grid=(b, s // ts, 2),
        in_specs=[
            pl.BlockSpec((1, ts, width), lambda bi, si, j: (bi, si, cb + j)),
            pl.BlockSpec((ts, LANES), lambda bi, si, j: (si, 0)),
            pl.BlockSpec((ts, LANES), lambda bi, si, j: (si, 0)),
            pl.BlockSpec((1, 1, LANES), lambda bi, si, j: (j, 0, 0)),
            pl.BlockSpec((LANES, LANES), lambda bi, si, j: (0, 0)),
        ],
        out_specs=pl.BlockSpec((1, ts, width), lambda bi, si, j: (bi, si, j)),
        out_shape=jax.ShapeDtypeStruct((b, s, 2 * width), BF16),
        compiler_params=_cparams(("parallel", "parallel", "parallel")),
        name="qk_prep",
    )(proj, cos_t, sin_t, g, gmat)


def _diff_attn_kernel(q_ref, k_ref, v_ref, lamv_ref, sub_ref, o_ref,
                      m1, l1, a1, m2, l2, a2, *, lam_init):
    ki = pl.program_id(3)

    @pl.when(ki == 0)
    def _():
        for m, l, a in ((m1, l1, a1), (m2, l2, a2)):
            m[...] = jnp.full(m.shape, -jnp.inf, F32)
            l[...] = jnp.zeros(l.shape, F32)
            a[...] = jnp.zeros(a.shape, F32)

    q = q_ref[0]
    k = k_ref[0]
    v = v_ref[0].astype(BF16)
    lane = lax.broadcasted_iota(jnp.int32, q.shape, 1)
    zero = jnp.zeros_like(q)
    q_parts = (jnp.where(lane < DA_QK_DIM, q, zero), jnp.where(lane >= DA_QK_DIM, q, zero))
    for qq, m, l, a in ((q_parts[0], m1, l1, a1), (q_parts[1], m2, l2, a2)):
        s = lax.dot_general(qq, k, (((1,), (1,)), ((), ())), preferred_element_type=F32)
        m_old = m[...]
        m_new = jnp.maximum(m_old, jnp.max(s, axis=-1, keepdims=True))
        alpha = jnp.exp(m_old - m_new)
        p = jnp.exp(s - m_new)
        l[...] = alpha * l[...] + jnp.sum(p, axis=-1, keepdims=True)
        a[...] = alpha * a[...] + jnp.dot(p.astype(BF16), v, preferred_element_type=F32)
        m[...] = m_new

    @pl.when(ki == pl.num_programs(3) - 1)
    def _():
        lv = lamv_ref[...]
        lam = (jnp.exp(jnp.sum(lv[0:1] * lv[1:2], axis=-1, keepdims=True))
               - jnp.exp(jnp.sum(lv[2:3] * lv[3:4], axis=-1, keepdims=True)) + lam_init)
        o = a1[...] / l1[...] - lam * (a2[...] / l2[...])
        ms = jnp.mean(o * o, axis=-1, keepdims=True)
        y = o * lax.rsqrt(ms + EPS) * sub_ref[0]
        o_ref[0] = (y * (1.0 - lam_init)).astype(o_ref.dtype)


def diff_attention(qk, proj, v_col0, lamv, subln, lam_init, tq=512, tk=1024):
    b, s, two_a = qk.shape
    nh = two_a // 2 // HEAD_DIM
    tq = _pick(s, tq)
    tk = _pick(s, tk)
    vb = v_col0 // HEAD_DIM
    return pl.pallas_call(
        functools.partial(_diff_attn_kernel, lam_init=lam_init),
        grid=(b, nh, s // tq, s // tk),
        in_specs=[
            pl.BlockSpec((1, tq, HEAD_DIM), lambda bi, h, qi, ki: (bi, qi, h)),
            pl.BlockSpec((1, tk, HEAD_DIM), lambda bi, h, qi, ki: (bi, ki, nh + h)),
            pl.BlockSpec((1, tk, HEAD_DIM), lambda bi, h, qi, ki: (bi, ki, vb + h)),
            pl.BlockSpec((4, DA_QK_DIM), lambda bi, h, qi, ki: (0, 0)),
            pl.BlockSpec((1, 1, HEAD_DIM), lambda bi, h, qi, ki: (h, 0, 0)),
        ],
        out_specs=pl.BlockSpec((1, tq, HEAD_DIM), lambda bi, h, qi, ki: (bi, qi, h)),
        out_shape=jax.ShapeDtypeStruct((b, s, nh * HEAD_DIM), BF16),
        scratch_shapes=[pltpu.VMEM((tq, 1), F32), pltpu.VMEM((tq, 1), F32), pltpu.VMEM((tq, HEAD_DIM), F32),
                        pltpu.VMEM((tq, 1), F32), pltpu.VMEM((tq, 1), F32), pltpu.VMEM((tq, HEAD_DIM), F32)],
        compiler_params=_cparams(("parallel", "parallel", "parallel", "arbitrary")),
        name="diff_attention",
    )(qk, qk, proj, lamv, subln.reshape(nh, 1, HEAD_DIM))


def _retention_consts(nh, reverse):
    c = RET_CHUNK
    log_g = np.log1p(-(2.0 ** (-5.0 - np.arange(nh, dtype=np.float64))))
    pos = np.arange(c, dtype=np.float64)
    diff = pos[:, None] - pos[None, :]
    if reverse:
        mask = diff < 0
        dmat = np.where(mask[None], np.exp(np.where(mask, -diff, 0.0)[None] * log_g[:, None, None]), 0.0)
        qdec = np.exp((c - pos)[None, :] * log_g[:, None])
        kdec = np.exp(pos[None, :] * log_g[:, None])
    else:
        mask = diff >= 0
        dmat = np.where(mask[None], np.exp(np.where(mask, diff, 0.0)[None] * log_g[:, None, None]), 0.0)
        qdec = np.exp((pos + 1.0)[None, :] * log_g[:, None])
        kdec = np.exp((c - 1 - pos)[None, :] * log_g[:, None])
    cdec = np.exp(c * log_g)
    bc = lambda a: jnp.asarray(np.broadcast_to(a[:, :, None], (nh, c, HEAD_DIM)).astype(np.float32))
    return jnp.asarray(dmat.astype(np.float32)), bc(qdec), bc(kdec), [float(x) for x in cdec]


def _retention_kernel(*refs, reverse, cdec, nh):
    if reverse:
        q_ref, k_ref, v_ref, d_ref, qd_ref, kd_ref, f_ref, gb_ref, nw_ref, o_ref, r_ref = refs
    else:
        q_ref, k_ref, v_ref, d_ref, qd_ref, kd_ref, o_ref, r_ref = refs

    @pl.when(pl.program_id(1) == 0)
    def _():
        r_ref[...] = jnp.zeros(r_ref.shape, F32)

    c = RET_CHUNK
    nchunk = q_ref.shape[1] // c
    order = range(nchunk - 1, -1, -1) if reverse else range(nchunk)
    for h in range(nh):
        cols = slice(h * HEAD_DIM, (h + 1) * HEAD_DIM)
        dmat = d_ref[h]
        qd = qd_ref[h]
        kd = kd_ref[h]
        for ci in order:
            rows = slice(ci * c, (ci + 1) * c)
            q = q_ref[0, rows, cols]
            k = k_ref[0, rows, cols]
            v = v_ref[0, rows, cols]
            s = lax.dot_general(q, k, (((1,), (1,)), ((), ())), preferred_element_type=F32) * dmat
            inner = jnp.dot(s.astype(BF16), v.astype(BF16), preferred_element_type=F32)
            r_prev = r_ref[h]
            cross = jnp.dot(q, r_prev.astype(BF16), preferred_element_type=F32) * qd
            kv = lax.dot_general(k, (v * kd).astype(BF16), (((0,), (0,)), ((), ())),
                                 preferred_element_type=F32)
            r_ref[h] = r_prev * cdec[h] + kv
            out = inner + cross
            if reverse:
                ret = out + f_ref[0, rows, cols]
                ms = jnp.mean(ret * ret, axis=-1, keepdims=True)
                y = ret * lax.rsqrt(ms + EPS) * nw_ref[h]
                gate = gb_ref[0, rows, cols]
                y = y * (gate * jax.nn.sigmoid(gate))
                o_ref[0, rows, cols] = y.astype(o_ref.dtype)
            else:
                o_ref[0, rows, cols] = out


def retention(qk, proj, v_col0, g_col0, ret_norm, tr=512):
    b, s, two_w = qk.shape
    w = two_w // 2
    nh = w // HEAD_DIM
    tr = _pick(s, tr)
    nt = s // tr
    vb, gbk = v_col0 // w, g_col0 // w
    outs = None
    for reverse in (False, True):
        dmat, qd, kd, cdec = _retention_consts(nh, reverse)
        tmap = (lambda t: nt - 1 - t) if reverse else (lambda t: t)
        row = lambda cb: (lambda bi, t: (bi, tmap(t), cb))
        const3 = lambda bi, t: (0, 0, 0)
        in_specs = [
            pl.BlockSpec((1, tr, w), row(0)),
            pl.BlockSpec((1, tr, w), row(1)),
            pl.BlockSpec((1, tr, w), row(vb)),
            pl.BlockSpec((nh, RET_CHUNK, RET_CHUNK), const3),
            pl.BlockSpec((nh, RET_CHUNK, HEAD_DIM), const3),
            pl.BlockSpec((nh, RET_CHUNK, HEAD_DIM), const3),
        ]
        args = [qk, qk, proj, dmat, qd, kd]
        if reverse:
            in_specs += [pl.BlockSpec((1, tr, w), row(0)), pl.BlockSpec((1, tr, w), row(gbk)),
                         pl.BlockSpec((nh, 1, HEAD_DIM), const3)]
            args += [outs, proj, ret_norm.astype(F32).reshape(nh, 1, HEAD_DIM)]
        outs = pl.pallas_call(
            functools.partial(_retention_kernel, reverse=reverse, cdec=cdec, nh=nh),
            grid=(b, nt),
            in_specs=in_specs,
            out_specs=pl.BlockSpec((1, tr, w), row(0)),
            out_shape=jax.ShapeDtypeStruct((b, s, w), BF16 if reverse else F32),
            scratch_shapes=[pltpu.VMEM((nh, HEAD_DIM, HEAD_DIM), F32)],
            compiler_params=_cparams(("parallel", "arbitrary")),
            name="retention_bwd" if reverse else "retention_fwd",
        )(*args)
    return outs


def _rg_slab_plan(d_rnn, bs):
    tiles = []
    c0 = 0
    while c0 < d_rnn:
        c1 = min(c0 + MXU_DIM, d_rnn)
        r0 = (c0 // bs) * bs
        r1 = ((c1 - 1) // bs + 1) * bs
        k0 = (r0 // LANES) * LANES
        k1 = min(-(-r1 // LANES) * LANES, d_rnn)
        tiles.append((c0, c1, k0, k1))
        c0 = c1
    kmax = max(k1 - k0 for _, _, k0, k1 in tiles)
    plan = [(c0, c1, min(k0, d_rnn - kmax)) for c0, c1, k0, _ in tiles]
    return plan, kmax


def _rg_slabs(w_blocks, plan, kmax):
    dense = jax.scipy.linalg.block_diag(*[w_blocks[i] for i in range(w_blocks.shape[0])])
    slabs = []
    for c0, c1, k0 in plan:
        sl = dense[k0:k0 + kmax, c0:c1]
        if c1 - c0 < MXU_DIM:
            sl = jnp.pad(sl, ((0, 0), (0, MXU_DIM - (c1 - c0))))
        slabs.append(sl)
    return jnp.stack(slabs).astype(BF16)


def _rg_kernel(*refs, reverse, plan, kmax):
    if reverse:
        (x_ref, xp_ref, xn_ref, cw_ref, cb_ref, wa_ref, wi_ref, ba_ref, bi_ref, lam_ref,
         g_ref, hf_ref, o_ref, a_s, b_s, carry) = refs
    else:
        (x_ref, xp_ref, xn_ref, cw_ref, cb_ref, wa_ref, wi_ref, ba_ref, bi_ref, lam_ref,
         o_ref, a_s, b_s, carry) = refs
    t = pl.program_id(1)
    nt = pl.num_programs(1)
    tchunk = (nt - 1 - t) if reverse else t

    @pl.when(t == 0)
    def _():
        carry[...] = jnp.zeros(carry.shape, F32)

    ts, c = x_ref.shape[1], x_ref.shape[2]
    xm = x_ref[0]
    xp = jnp.where(tchunk == 0, 0.0, xp_ref[0])
    xn = jnp.where(tchunk == nt - 1, 0.0, xn_ref[0])
    xe = jnp.concatenate([xp, xm, xn], axis=0)
    ext = ts + 2 * SUBLANES
    win = slice(SUBLANES, SUBLANES + ts)
    cw = cw_ref[...]
    xc = (cw[0:1] * pltpu.roll(xe, 2, axis=0)[win] + cw[1:2] * pltpu.roll(xe, 1, axis=0)[win]
          + cw[2:3] * xm + cw[3:4] * pltpu.roll(xe, ext - 1, axis=0)[win] + cb_ref[...])
    xb = xc.astype(BF16)
    lam = lam_ref[...]
    sp = jnp.maximum(-lam, 0.0) + jnp.log1p(jnp.exp(-jnp.abs(lam)))
    for ti, (c0, c1, k0) in enumerate(plan):
        xs = xb[:, k0:k0 + kmax]
        wdt = c1 - c0
        r = jax.nn.sigmoid(jnp.dot(xs, wa_ref[ti], preferred_element_type=F32)[:, :wdt] + ba_ref[:, c0:c1])
        i = jax.nn.sigmoid(jnp.dot(xs, wi_ref[ti], preferred_element_type=F32)[:, :wdt] + bi_ref[:, c0:c1])
        log_a = (-RG_C) * sp[:, c0:c1] * r
        a = jnp.exp(log_a)
        a_s[:, c0:c1] = a
        b_s[:, c0:c1] = jnp.sqrt(1.0 - a * a) * (i * xc[:, c0:c1])

    ngroups = ts // SUBLANES
    rowid = lax.broadcasted_iota(jnp.int32, (SUBLANES, c), 0)

    def group(gi, h):
        g = (ngroups - 1 - gi) if reverse else gi
        rows = pl.ds(pl.multiple_of(g * SUBLANES, SUBLANES), SUBLANES)
        a = a_s[rows, :]
        b = b_s[rows, :]
        for k in (1, 2, 4):
            if reverse:
                valid = rowid < SUBLANES - k
                sh = SUBLANES - k
            else:
                valid = rowid >= k
                sh = k
            b = b + a * jnp.where(valid, pltpu.roll(b, sh, axis=0), 0.0)
            a = a * jnp.where(valid, pltpu.roll(a, sh, axis=0), 1.0)
        hg = b + a * h
        b_s[rows, :] = hg
        edge = hg[0:1] if reverse else hg[SUBLANES - 1:SUBLANES]
        return jnp.broadcast_to(edge, (SUBLANES, c))

    carry[...] = lax.fori_loop(0, ngroups, group, carry[...])
    if reverse:
        gate = g_ref[0]
        o_ref[0] = (jax.nn.gelu(gate) * (hf_ref[0] + b_s[...])).astype(o_ref.dtype)
    else:
        o_ref[0] = b_s[...]


def rg_lru_block(proj, conv_w, conv_b, wa, ba, wi, bi, lam, ts=256):
    b, s, two_c = proj.shape
    c = two_c // 2
    nb, bs = wa.shape[1], wa.shape[2]
    ts = _pick(s, ts)
    nt = s // ts
    hb = ts // SUBLANES
    plan, kmax = _rg_slab_plan(c, bs)
    out = None
    for reverse in (False, True):
        d = 1 if reverse else 0
        tmap = (lambda t: nt - 1 - t) if reverse else (lambda t: t)
        const2 = lambda bi_, t: (0, 0)
        const3 = lambda bi_, t: (0, 0, 0)
        in_specs = [
            pl.BlockSpec((1, ts, c), lambda bi_, t: (bi_, tmap(t), 1)),
            pl.BlockSpec((1, SUBLANES, c), lambda bi_, t: (bi_, jnp.maximum(tmap(t) * hb - 1, 0), 1)),
            pl.BlockSpec((1, SUBLANES, c), lambda bi_, t: (bi_, jnp.minimum((tmap(t) + 1) * hb, nt * hb - 1), 1)),
            pl.BlockSpec((CONV_WIDTH, c), const2),
            pl.BlockSpec((1, c), const2),
            pl.BlockSpec((len(plan), kmax, MXU_DIM), const3),
            pl.BlockSpec((len(plan), kmax, MXU_DIM), const3),
            pl.BlockSpec((1, c), const2),
            pl.BlockSpec((1, c), const2),
            pl.BlockSpec((1, c), const2),
        ]
        args = [proj, proj, proj, conv_w.astype(F32), conv_b.astype(F32).reshape(1, c),
                _rg_slabs(wa[d], plan, kmax), _rg_slabs(wi[d], plan, kmax),
                ba[d].astype(F32).reshape(1, c), bi[d].astype(F32).reshape(1, c), lam[d].astype(F32).reshape(1, c)]
        if reverse:
            in_specs += [pl.BlockSpec((1, ts, c), lambda bi_, t: (bi_, tmap(t), 0)),
                         pl.BlockSpec((1, ts, c), lambda bi_, t: (bi_, tmap(t), 0))]
            args += [proj, out]
        out = pl.pallas_call(
            functools.partial(_rg_kernel, reverse=reverse, plan=plan, kmax=kmax),
            grid=(b, nt),
            in_specs=in_specs,
            out_specs=pl.BlockSpec((1, ts, c), lambda bi_, t: (bi_, tmap(t), 0)),
            out_shape=jax.ShapeDtypeStruct((b, s, c), BF16 if reverse else F32),
            scratch_shapes=[pltpu.VMEM((ts, c), F32), pltpu.VMEM((ts, c), F32), pltpu.VMEM((SUBLANES, c), F32)],
            compiler_params=_cparams(("parallel", "arbitrary")),
            name="rg_lru_bwd" if reverse else "rg_lru_fwd",
        )(*args)
    return out


def _router_kernel(x_ref, g_ref, r_ref, xn_ref, aff_ref):
    x = x_ref[...]
    ms = jnp.mean(x * x, axis=-1, keepdims=True)
    xn = x * lax.rsqrt(ms + EPS) * g_ref[...]
    xn_ref[...] = xn.astype(BF16)
    logits = lax.dot_general(r_ref[...], xn, (((1,), (1,)), ((), ())), precision=lax.Precision.HIGHEST,
                             preferred_element_type=F32)
    z = logits - jnp.max(logits, axis=0, keepdims=True)
    e = jnp.exp(z)
    aff_ref[...] = e / jnp.sum(e, axis=0, keepdims=True)


def router(x, g, router_w, tm=512):
    n, d = x.shape
    e = router_w.shape[1]
    tm = _pick(n, tm)
    return pl.pallas_call(
        _router_kernel,
        grid=(n // tm,),
        in_specs=[pl.BlockSpec((tm, d), lambda i: (i, 0)), pl.BlockSpec((1, d), lambda i: (0, 0)),
                  pl.BlockSpec((e, d), lambda i: (0, 0))],
        out_specs=[pl.BlockSpec((tm, d), lambda i: (i, 0)), pl.BlockSpec((e, tm), lambda i: (0, i))],
        out_shape=[jax.ShapeDtypeStruct((n, d), BF16), jax.ShapeDtypeStruct((e, n), F32)],
        compiler_params=_cparams(("parallel",)),
        name="router",
    )(x, g.reshape(1, d), router_w.T.astype(F32))


def _expert_ffn_kernel(x_ref, gate_ref, wg_ref, wu_ref, wd_ref, o_ref, acc_ref):
    f = pl.program_id(2)

    @pl.when(f == 0)
    def _():
        acc_ref[...] = jnp.zeros(acc_ref.shape, F32)

    x = x_ref[...]
    g = jnp.dot(x, wg_ref[0], preferred_element_type=F32)
    u = jnp.dot(x, wu_ref[0], preferred_element_type=F32)
    hdn = (g * jax.nn.sigmoid(g) * u).astype(BF16)
    acc_ref[...] += jnp.dot(hdn, wd_ref[0], preferred_element_type=F32)

    @pl.when(f == pl.num_programs(2) - 1)
    def _():
        o_ref[...] = acc_ref[...] * gate_ref[...]


def expert_ffn(xe, gates, wg, wu, wd, cap, tm=512, tf=512):
    ne, d, ff = wg.shape
    tm = _pick(cap, tm)
    tf = _pick(ff, tf)
    nm = cap // tm
    return pl.pallas_call(
        _expert_ffn_kernel,
        grid=(ne, nm, ff // tf),
        in_specs=[
            pl.BlockSpec((tm, d), lambda e, i, f: (e * nm + i, 0)),
            pl.BlockSpec((tm, 1), lambda e, i, f: (e * nm + i, 0)),
            pl.BlockSpec((1, d, tf), lambda e, i, f: (e, 0, f)),
            pl.BlockSpec((1, d, tf), lambda e, i, f: (e, 0, f)),
            pl.BlockSpec((1, tf, d), lambda e, i, f: (e, f, 0)),
        ],
        out_specs=pl.BlockSpec((tm, d), lambda e, i, f: (e * nm + i, 0)),
        out_shape=jax.ShapeDtypeStruct((ne * cap, d), F32),
        scratch_shapes=[pltpu.VMEM((tm, d), F32)],
        compiler_params=_cparams(("parallel", "parallel", "arbitrary")),
        name="expert_ffn",
    )(xe, gates, wg, wu, wd)


def ec_moe(x, g, router_w, wg, wu, wd):
    n, d = x.shape
    ne = router_w.shape[1]
    cap = max(1, 2 * n // ne)
    xn, aff_t = router(x, g, router_w)
    gates, idx = lax.top_k(aff_t, cap)
    flat = idx.reshape(-1)
    xe = xn[flat]
    outs = expert_ffn(xe, gates.reshape(-1, 1), wg, wu, wd, cap)
    return x.at[flat].add(outs)


def _trunk(x, p):
    b, s, d = x.shape
    n = b * s
    x = x.reshape(n, d)
    depth = p["norm_mix"].shape[0]
    for layer in range(depth):
        j = layer // 2
        if layer % 2 == 0:
            w_in = p["att_w_in"][j]
            a_qk = (w_in.shape[1] // 7)
            proj = norm_matmul(x, p["norm_mix"][layer], w_in).reshape(b, s, -1)
            lam_init = 0.8 - 0.6 * float(np.exp(-0.3 * layer))
            qk_a = qk_prep(proj, 0, a_qk, DA_QK_DIM, (p["att_q_norm"][j], p["att_k_norm"][j]),
                           (DA_QK_DIM ** -0.5, 1.0))
            lamv = jnp.stack([p["att_lam_q1"][j], p["att_lam_k1"][j], p["att_lam_q2"][j],
                              p["att_lam_k2"][j]]).astype(F32)
            out_a = diff_attention(qk_a, proj, 2 * a_qk, lamv, p["att_subln"][j].astype(F32), lam_init)
            qk_b = qk_prep(proj, 3 * a_qk, a_qk, HEAD_DIM, None, (1.0, HEAD_DIM ** -0.5))
            out_b = retention(qk_b, proj, 5 * a_qk, 6 * a_qk, p["ret_norm"][j])
            w_out = p["att_w_out"][j]
            x = matmul_residual([(out_a.reshape(n, -1), w_out[:a_qk]), (out_b.reshape(n, -1), w_out[a_qk:])], x)
        else:
            proj = norm_matmul(x, p["norm_mix"][layer], p["rg_w_in"][j]).reshape(b, s, -1)
            y = rg_lru_block(proj, p["rg_conv_w"][j], p["rg_conv_b"][j], p["rg_wa"][j], p["rg_ba"][j],
                             p["rg_wi"][j], p["rg_bi"][j], p["rg_lambda"][j])
            x = matmul_residual([(y.reshape(n, -1), p["rg_w_out"][j])], x)
        x = ec_moe(x, p["norm_ffn"][layer], p["moe_router"][layer], p["moe_w_gate"][layer],
                   p["moe_w_up"][layer], p["moe_w_down"][layer])
    return x.reshape(b, s, d)


def kernel(x_prompt, x_sample, norm_mix, norm_ffn, att_w_in, att_w_out, att_q_norm, att_k_norm, att_lam_q1, att_lam_k1, att_lam_q2, att_lam_k2, att_subln, ret_norm, rg_w_in, rg_conv_w, rg_conv_b, rg_wa, rg_ba, rg_wi, rg_bi, rg_lambda, rg_w_out, moe_router, moe_w_gate, moe_w_up, moe_w_down):
    p = dict(
        norm_mix=norm_mix.astype(F32), norm_ffn=norm_ffn.astype(F32),
        att_w_in=att_w_in.astype(BF16), att_w_out=att_w_out.astype(BF16),
        att_q_norm=att_q_norm, att_k_norm=att_k_norm,
        att_lam_q1=att_lam_q1, att_lam_k1=att_lam_k1, att_lam_q2=att_lam_q2, att_lam_k2=att_lam_k2,
        att_subln=att_subln, ret_norm=ret_norm,
        rg_w_in=rg_w_in.astype(BF16), rg_conv_w=rg_conv_w, rg_conv_b=rg_conv_b,
        rg_wa=rg_wa, rg_ba=rg_ba, rg_wi=rg_wi, rg_bi=rg_bi, rg_lambda=rg_lambda,
        rg_w_out=rg_w_out.astype(BF16), moe_router=moe_router,
        moe_w_gate=moe_w_gate.astype(BF16), moe_w_up=moe_w_up.astype(BF16), moe_w_down=moe_w_down.astype(BF16),
    )
    return (_trunk(x_prompt, p), _trunk(x_sample, p))
```

```python
import functools
import math

import numpy as np
import jax
import jax.numpy as jnp
from jax import lax
from jax.experimental import pallas as pl
from jax.experimental.pallas import tpu as pltpu

F32 = jnp.float32
BF16 = jnp.bfloat16

HEAD_DIM = 128
DA_QK_DIM = HEAD_DIM // 2
EPS = 1e-6
ROPE_THETA = 10000.0
RG_C = 8.0
CONV_WIDTH = 4
RET_CHUNK = 256
LANES = 128
SUBLANES = 8
HALO = 16
MXU_DIM = 256
VMEM_LIMIT = 52 * 1024 * 1024


def _cparams(sem):
    return pltpu.CompilerParams(dimension_semantics=sem, vmem_limit_bytes=VMEM_LIMIT)


def _pick(n, pref):
    t = min(pref, n)
    while n % t:
        t //= 2
    return t


def _norm_matmul_kernel(x_ref, g_ref, w_ref, o_ref, xn_ref):
    @pl.when(pl.program_id(1) == 0)
    def _():
        x = x_ref[...]
        ms = jnp.mean(x * x, axis=-1, keepdims=True)
        xn_ref[...] = (x * lax.rsqrt(ms + EPS) * g_ref[...]).astype(BF16)

    o_ref[...] = jnp.dot(xn_ref[...], w_ref[...], preferred_element_type=F32).astype(o_ref.dtype)


def norm_matmul(x, g, w, out_dtype=F32, tm=1024, tn=1024):
    n, d = x.shape
    nout = w.shape[1]
    tm = _pick(n, tm)
    tn = _pick(nout, tn)
    return pl.pallas_call(
        _norm_matmul_kernel,
        grid=(n // tm, nout // tn),
        in_specs=[
            pl.BlockSpec((tm, d), lambda i, j: (i, 0)),
            pl.BlockSpec((1, d), lambda i, j: (0, 0)),
            pl.BlockSpec((d, tn), lambda i, j: (0, j)),
        ],
        out_specs=pl.BlockSpec((tm, tn), lambda i, j: (i, j)),
        out_shape=jax.ShapeDtypeStruct((n, nout), out_dtype),
        scratch_shapes=[pltpu.VMEM((tm, d), BF16)],
        compiler_params=_cparams(("parallel", "arbitrary")),
        name="norm_matmul",
    )(x, g.reshape(1, d), w)


def _matmul_res_kernel(*refs, npairs):
    r_ref, o_ref = refs[2 * npairs], refs[2 * npairs + 1]
    acc = r_ref[...]
    for p in range(npairs):
        acc = acc + jnp.dot(refs[2 * p][...], refs[2 * p + 1][...], preferred_element_type=F32)
    o_ref[...] = acc


def matmul_residual(pairs, res, tm=512, tn=512):
    n, dout = res.shape
    tm = _pick(n, tm)
    tn = _pick(dout, tn)
    in_specs, args = [], []
    for y, w in pairs:
        k = y.shape[1]
        in_specs += [pl.BlockSpec((tm, k), lambda i, j: (i, 0)), pl.BlockSpec((k, tn), lambda i, j: (0, j))]
        args += [y, w]
    in_specs.append(pl.BlockSpec((tm, tn), lambda i, j: (i, j)))
    args.append(res)
    return pl.pallas_call(
        functools.partial(_matmul_res_kernel, npairs=len(pairs)),
        grid=(n // tm, dout // tn),
        in_specs=in_specs,
        out_specs=pl.BlockSpec((tm, tn), lambda i, j: (i, j)),
        out_shape=jax.ShapeDtypeStruct((n, dout), F32),
        compiler_params=_cparams(("parallel", "parallel")),
        name="matmul_residual",
    )(*args)


def _rope_tables(seq, group):
    half = group // 2
    inv = ROPE_THETA ** (-jnp.arange(half, dtype=F32) / half)
    ang = jnp.arange(seq, dtype=F32)[:, None] * inv[None, :]
    cos, sin = jnp.cos(ang), jnp.sin(ang)
    reps = LANES // group
    cos_t = jnp.tile(jnp.concatenate([cos, cos], axis=-1), (1, reps))
    sin_t = jnp.tile(jnp.concatenate([-sin, sin], axis=-1), (1, reps))
    return cos_t, sin_t


def _qk_prep_kernel(x_ref, cos_ref, sin_ref, g_ref, gmat_ref, o_ref, *, group, normed, scales):
    j = pl.program_id(2)
    scale = jnp.where(j == 0, scales[0], scales[1]).astype(F32)
    cos = cos_ref[...]
    sin = sin_ref[...]
    g = g_ref[0]
    nh = x_ref.shape[2] // LANES
    lane = lax.broadcasted_iota(jnp.int32, cos.shape, 1)
    first_half = (lane % group) < (group // 2)
    for h in range(nh):
        x = x_ref[0, :, h * LANES:(h + 1) * LANES].astype(F32)
        if normed:
            sq = x * x
            hi = sq.astype(BF16)
            lo = (sq - hi.astype(F32)).astype(BF16)
            ms = (jnp.dot(hi, gmat_ref[...], preferred_element_type=F32)
                  + jnp.dot(lo, gmat_ref[...], preferred_element_type=F32))
            x = x * lax.rsqrt(ms + EPS) * g
        if group == LANES:
            swapped = pltpu.roll(x, LANES // 2, axis=1)
        else:
            fwd = pltpu.roll(x, group // 2, axis=1)
            bwd = pltpu.roll(x, LANES - group // 2, axis=1)
            swapped = jnp.where(first_half, bwd, fwd)
        y = (x * cos + swapped * sin) * scale
        o_ref[0, :, h * LANES:(h + 1) * LANES] = y.astype(o_ref.dtype)


def qk_prep(proj, col0, width, group, norm_w, scales, ts=512):
    b, s, _ = proj.shape
    ts = _pick(s, ts)
    cos_t, sin_t = _rope_tables(s, group)
    normed = norm_w is not None
    if normed:
        g = jnp.stack([jnp.tile(w.astype(F32), LANES // group) for w in norm_w]).reshape(2, 1, LANES)
    else:
        g = jnp.ones((2, 1, LANES), F32)
    grp = np.arange(LANES) // group
    gmat = jnp.asarray((grp[:, None] == grp[None, :]).astype(np.float32) / group, BF16)
    cb = col0 // width
    return pl.pallas_call(
        functools.partial(_qk_prep_kernel, group=group, normed=normed, scales=scales),
        grid=(b, s // ts, 2),
        in_specs=[
            pl.BlockSpec((1, ts, width), lambda bi, si, j: (bi, si, cb + j)),
            pl.BlockSpec((ts, LANES), lambda bi, si, j: (si, 0)),
            pl.BlockSpec((ts, LANES), lambda bi, si, j: (si, 0)),
            pl.BlockSpec((1, 1, LANES), lambda bi, si, j: (j, 0, 0)),
            pl.BlockSpec((LANES, LANES), lambda bi, si, j: (0, 0)),
        ],
        out_specs=pl.BlockSpec((1, ts, width), lambda bi, si, j: (bi, si, j)),
        out_shape=jax.ShapeDtypeStruct((b, s, 2 * width), BF16),
        compiler_params=_cparams(("parallel", "parallel", "parallel")),
        name="qk_prep",
    )(proj, cos_t, sin_t, g, gmat)


def _diff_attn_kernel(q_ref, k_ref, v_ref, lamv_ref, sub_ref, o_ref, m1, a1, m2, a2, *, lam_init):
    ki = pl.program_id(3)

    @pl.when(ki == 0)
    def _():
        for m, a in ((m1, a1), (m2, a2)):
            m[...] = jnp.full(m.shape, -jnp.inf, F32)
            a[...] = jnp.zeros(a.shape, F32)

    q = q_ref[0]
    k = k_ref[0]
    v = v_ref[0].astype(BF16)
    vlane = lax.broadcasted_iota(jnp.int32, v.shape, 1)
    v_ext = jnp.concatenate([v, jnp.where(vlane == 0, 1.0, 0.0).astype(BF16)], axis=1)
    lane = lax.broadcasted_iota(jnp.int32, q.shape, 1)
    zero = jnp.zeros_like(q)
    q_parts = (jnp.where(lane < DA_QK_DIM, q, zero), jnp.where(lane >= DA_QK_DIM, q, zero))
    for qq, m, a in ((q_parts[0], m1, a1), (q_parts[1], m2, a2)):
        s = lax.dot_general(qq, k, (((1,), (1,)), ((), ())), preferred_element_type=F32)
        m_old = m[...]
        m_new = jnp.maximum(m_old, jnp.max(s, axis=-1, keepdims=True))
        p = jnp.exp2(s - m_new).astype(BF16)
        a[...] = jnp.exp2(m_old - m_new) * a[...] + jnp.dot(p, v_ext, preferred_element_type=F32)
        m[...] = m_new

    @pl.when(ki == pl.num_programs(3) - 1)
    def _():
        lv = lamv_ref[...]
        lam = (jnp.exp(jnp.sum(lv[0:1] * lv[1:2], axis=-1, keepdims=True))
               - jnp.exp(jnp.sum(lv[2:3] * lv[3:4], axis=-1, keepdims=True)) + lam_init)
        acc1, acc2 = a1[...], a2[...]
        o = (acc1[:, :HEAD_DIM] / acc1[:, HEAD_DIM:HEAD_DIM + 1]
             - lam * (acc2[:, :HEAD_DIM] / acc2[:, HEAD_DIM:HEAD_DIM + 1]))
        ms = jnp.mean(o * o, axis=-1, keepdims=True)
        y = o * lax.rsqrt(ms + EPS) * sub_ref[0]
        o_ref[0] = (y * (1.0 - lam_init)).astype(o_ref.dtype)


def diff_attention(qk, proj, v_col0, lamv, subln, lam_init, tq=1024, tk=1024):
    b, s, two_a = qk.shape
    nh = two_a // 2 // HEAD_DIM
    tq = _pick(s, tq)
    tk = _pick(s, tk)
    vb = v_col0 // HEAD_DIM
    return pl.pallas_call(
        functools.partial(_diff_attn_kernel, lam_init=lam_init),
        grid=(b, nh, s // tq, s // tk),
        in_specs=[
            pl.BlockSpec((1, tq, HEAD_DIM), lambda bi, h, qi, ki: (bi, qi, h)),
            pl.BlockSpec((1, tk, HEAD_DIM), lambda bi, h, qi, ki: (bi, ki, nh + h)),
            pl.BlockSpec((1, tk, HEAD_DIM), lambda bi, h, qi, ki: (bi, ki, vb + h)),
            pl.BlockSpec((4, DA_QK_DIM), lambda bi, h, qi, ki: (0, 0)),
            pl.BlockSpec((1, 1, HEAD_DIM), lambda bi, h, qi, ki: (h, 0, 0)),
        ],
        out_specs=pl.BlockSpec((1, tq, HEAD_DIM), lambda bi, h, qi, ki: (bi, qi, h)),
        out_shape=jax.ShapeDtypeStruct((b, s, nh * HEAD_DIM), BF16),
        scratch_shapes=[pltpu.VMEM((tq, 1), F32), pltpu.VMEM((tq, 2 * HEAD_DIM), F32),
                        pltpu.VMEM((tq, 1), F32), pltpu.VMEM((tq, 2 * HEAD_DIM), F32)],
        compiler_params=_cparams(("parallel", "parallel", "parallel", "arbitrary")),
        name="diff_attention",
    )(qk, qk, proj, lamv, subln.reshape(nh, 1, HEAD_DIM))


def _retention_consts(nh, reverse):
    c = RET_CHUNK
    log_g = np.log1p(-(2.0 ** (-5.0 - np.arange(nh, dtype=np.float64))))
    pos = np.arange(c, dtype=np.float64)
    diff = pos[:, None] - pos[None, :]
    if reverse:
        mask = diff < 0
        dmat = np.where(mask[None], np.exp(np.where(mask, -diff, 0.0)[None] * log_g[:, None, None]), 0.0)
        qdec = np.exp((c - pos)[None, :] * log_g[:, None])
        kdec = np.exp(pos[None, :] * log_g[:, None])
    else:
        mask = diff >= 0
        dmat = np.where(mask[None], np.exp(np.where(mask, diff, 0.0)[None] * log_g[:, None, None]), 0.0)
        qdec = np.exp((pos + 1.0)[None, :] * log_g[:, None])
        kdec = np.exp((c - 1 - pos)[None, :] * log_g[:, None])
    cdec = np.exp(c * log_g)
    bc = lambda a: jnp.asarray(np.broadcast_to(a[:, :, None], (nh, c, HEAD_DIM)).astype(np.float32))
    return jnp.asarray(dmat.astype(np.float32)), bc(qdec), bc(kdec), [float(x) for x in cdec]


def _retention_kernel(*refs, reverse, cdec, nh):
    if reverse:
        q_ref, k_ref, v_ref, d_ref, qd_ref, kd_ref, f_ref, gb_ref, nw_ref, o_ref, r_ref = refs
    else:
        q_ref, k_ref, v_ref, d_ref, qd_ref, kd_ref, o_ref, r_ref = refs

    @pl.when(pl.program_id(1) == 0)
    def _():
        r_ref[...] = jnp.zeros(r_ref.shape, F32)

    c = RET_CHUNK
    nchunk = q_ref.shape[1] // c
    order = range(nchunk - 1, -1, -1) if reverse else range(nchunk)
    for h in range(nh):
        cols = slice(h * HEAD_DIM, (h + 1) * HEAD_DIM)
        dmat = d_ref[h]
        qd = qd_ref[h]
        kd = kd_ref[h]
        for ci in order:
            rows = slice(ci * c, (ci + 1) * c)
            q = q_ref[0, rows, cols]
            k = k_ref[0, rows, cols]
            v = v_ref[0, rows, cols].astype(F32)
            s = lax.dot_general(q, k, (((1,), (1,)), ((), ())), preferred_element_type=F32) * dmat
            inner = jnp.dot(s.astype(BF16), v.astype(BF16), preferred_element_type=F32)
            r_prev = r_ref[h]
            cross = jnp.dot(q, r_prev.astype(BF16), preferred_element_type=F32) * qd
            kv = lax.dot_general(k, (v * kd).astype(BF16), (((0,), (0,)), ((), ())),
                                 preferred_element_type=F32)
            r_ref[h] = r_prev * cdec[h] + kv
            out = inner + cross
            if reverse:
                ret = out + f_ref[0, rows, cols]
                ms = jnp.mean(ret * ret, axis=-1, keepdims=True)
                y = ret * lax.rsqrt(ms + EPS) * nw_ref[h]
                gate = gb_ref[0, rows, cols].astype(F32)
                y = y * (gate * jax.nn.sigmoid(gate))
                o_ref[0, rows, cols] = y.astype(o_ref.dtype)
            else:
                o_ref[0, rows, cols] = out


def retention(qk, proj, v_col0, g_col0, ret_norm, tr=512):
    b, s, two_w = qk.shape
    w = two_w // 2
    nh = w // HEAD_DIM
    tr = _pick(s, tr)
    nt = s // tr
    vb, gbk = v_col0 // w, g_col0 // w
    outs = None
    for reverse in (False, True):
        dmat, qd, kd, cdec = _retention_consts(nh, reverse)
        tmap = (lambda t: nt - 1 - t) if reverse else (lambda t: t)
        row = lambda cb: (lambda bi, t: (bi, tmap(t), cb))
        const3 = lambda bi, t: (0, 0, 0)
        in_specs = [
            pl.BlockSpec((1, tr, w), row(0)),
            pl.BlockSpec((1, tr, w), row(1)),
            pl.BlockSpec((1, tr, w), row(vb)),
            pl.BlockSpec((nh, RET_CHUNK, RET_CHUNK), const3),
            pl.BlockSpec((nh, RET_CHUNK, HEAD_DIM), const3),
            pl.BlockSpec((nh, RET_CHUNK, HEAD_DIM), const3),
        ]
        args = [qk, qk, proj, dmat, qd, kd]
        if reverse:
            in_specs += [pl.BlockSpec((1, tr, w), row(0)), pl.BlockSpec((1, tr, w), row(gbk)),
                         pl.BlockSpec((nh, 1, HEAD_DIM), const3)]
            args += [outs, proj, ret_norm.astype(F32).reshape(nh, 1, HEAD_DIM)]
        outs = pl.pallas_call(
            functools.partial(_retention_kernel, reverse=reverse, cdec=cdec, nh=nh),
            grid=(b, nt),
            in_specs=in_specs,
            out_specs=pl.BlockSpec((1, tr, w), row(0)),
            out_shape=jax.ShapeDtypeStruct((b, s, w), BF16 if reverse else F32),
            scratch_shapes=[pltpu.VMEM((nh, HEAD_DIM, HEAD_DIM), F32)],
            compiler_params=_cparams(("parallel", "arbitrary")),
            name="retention_bwd" if reverse else "retention_fwd",
        )(*args)
    return outs


def _rg_slab_plan(d_rnn, bs):
    tiles = []
    c0 = 0
    while c0 < d_rnn:
        c1 = min(c0 + MXU_DIM, d_rnn)
        r0 = (c0 // bs) * bs
        r1 = ((c1 - 1) // bs + 1) * bs
        k0 = (r0 // LANES) * LANES
        k1 = min(-(-r1 // LANES) * LANES, d_rnn)
        tiles.append((c0, c1, k0, k1))
        c0 = c1
    kmax = max(k1 - k0 for _, _, k0, k1 in tiles)
    plan = [(c0, c1, min(k0, d_rnn - kmax)) for c0, c1, k0, _ in tiles]
    return plan, kmax


def _rg_slabs(w_blocks, plan, kmax):
    dense = jax.scipy.linalg.block_diag(*[w_blocks[i] for i in range(w_blocks.shape[0])])
    slabs = []
    for c0, c1, k0 in plan:
        sl = dense[k0:k0 + kmax, c0:c1]
        if c1 - c0 < MXU_DIM:
            sl = jnp.pad(sl, ((0, 0), (0, MXU_DIM - (c1 - c0))))
        slabs.append(sl)
    return jnp.stack(slabs).astype(BF16)


def _rg_kernel(*refs, reverse, plan, kmax):
    if reverse:
        (x_ref, xp_ref, xn_ref, cw_ref, cb_ref, wa_ref, wi_ref, ba_ref, bi_ref, lam_ref,
         g_ref, hf_ref, o_ref, a_s, b_s, carry) = refs
    else:
        (x_ref, xp_ref, xn_ref, cw_ref, cb_ref, wa_ref, wi_ref, ba_ref, bi_ref, lam_ref,
         o_ref, a_s, b_s, carry) = refs
    t = pl.program_id(1)
    nt = pl.num_programs(1)
    tchunk = (nt - 1 - t) if reverse else t

    @pl.when(t == 0)
    def _():
        carry[...] = jnp.zeros(carry.shape, F32)

    ts, c = x_ref.shape[1], x_ref.shape[2]
    xm = x_ref[0].astype(F32)
    xp = jnp.where(tchunk == 0, 0.0, xp_ref[0].astype(F32))
    xn = jnp.where(tchunk == nt - 1, 0.0, xn_ref[0].astype(F32))
    xe = jnp.concatenate([xp, xm, xn], axis=0)
    ext = ts + 2 * HALO
    win = slice(HALO, HALO + ts)
    cw = cw_ref[...]
    xc = (cw[0:1] * pltpu.roll(xe, 2, axis=0)[win] + cw[1:2] * pltpu.roll(xe, 1, axis=0)[win]
          + cw[2:3] * xm + cw[3:4] * pltpu.roll(xe, ext - 1, axis=0)[win] + cb_ref[...])
    xb = xc.astype(BF16)
    lam = lam_ref[...]
    sp = jnp.maximum(-lam, 0.0) + jnp.log1p(jnp.exp(-jnp.abs(lam)))
    for ti, (c0, c1, k0) in enumerate(plan):
        xs = xb[:, k0:k0 + kmax]
        wdt = c1 - c0
        r = jax.nn.sigmoid(jnp.dot(xs, wa_ref[ti], preferred_element_type=F32)[:, :wdt] + ba_ref[:, c0:c1])
        i = jax.nn.sigmoid(jnp.dot(xs, wi_ref[ti], preferred_element_type=F32)[:, :wdt] + bi_ref[:, c0:c1])
        log_a = (-RG_C) * sp[:, c0:c1] * r
        a = jnp.exp(log_a)
        a_s[:, c0:c1] = a
        b_s[:, c0:c1] = jnp.sqrt(1.0 - a * a) * (i * xc[:, c0:c1])

    ngroups = ts // SUBLANES
    rowid = lax.broadcasted_iota(jnp.int32, (SUBLANES, c), 0)

    def group(gi, h):
        g = (ngroups - 1 - gi) if reverse else gi
        rows = pl.ds(pl.multiple_of(g * SUBLANES, SUBLANES), SUBLANES)
        a = a_s[rows, :]
        b = b_s[rows, :]
        for k in (1, 2, 4):
            if reverse:
                valid = rowid < SUBLANES - k
                sh = SUBLANES - k
            else:
                valid = rowid >= k
                sh = k
            b = b + a * jnp.where(valid, pltpu.roll(b, sh, axis=0), 0.0)
            a = a * jnp.where(valid, pltpu.roll(a, sh, axis=0), 1.0)
        hg = b + a * h
        b_s[rows, :] = hg
        edge = hg[0:1] if reverse else hg[SUBLANES - 1:SUBLANES]
        return jnp.broadcast_to(edge, (SUBLANES, c))

    carry[...] = lax.fori_loop(0, ngroups, group, carry[...])
    if reverse:
        gate = g_ref[0].astype(F32)
        o_ref[0] = (jax.nn.gelu(gate) * (hf_ref[0] + b_s[...])).astype(o_ref.dtype)
    else:
        o_ref[0] = b_s[...]


def rg_lru_block(proj, conv_w, conv_b, wa, ba, wi, bi, lam, ts=256):
    b, s, two_c = proj.shape
    c = two_c // 2
    nb, bs = wa.shape[1], wa.shape[2]
    ts = _pick(s, ts)
    nt = s // ts
    hb = ts // HALO
    plan, kmax = _rg_slab_plan(c, bs)
    out = None
    for reverse in (False, True):
        d = 1 if reverse else 0
        tmap = (lambda t: nt - 1 - t) if reverse else (lambda t: t)
        const2 = lambda bi_, t: (0, 0)
        const3 = lambda bi_, t: (0, 0, 0)
        in_specs = [
            pl.BlockSpec((1, ts, c), lambda bi_, t: (bi_, tmap(t), 1)),
            pl.BlockSpec((1, HALO, c), lambda bi_, t: (bi_, jnp.maximum(tmap(t) * hb - 1, 0), 1)),
            pl.BlockSpec((1, HALO, c), lambda bi_, t: (bi_, jnp.minimum((tmap(t) + 1) * hb, nt * hb - 1), 1)),
            pl.BlockSpec((CONV_WIDTH, c), const2),
            pl.BlockSpec((1, c), const2),
            pl.BlockSpec((len(plan), kmax, MXU_DIM), const3),
            pl.BlockSpec((len(plan), kmax, MXU_DIM), const3),
            pl.BlockSpec((1, c), const2),
            pl.BlockSpec((1, c), const2),
            pl.BlockSpec((1, c), const2),
        ]
        args = [proj, proj, proj, conv_w.astype(F32), conv_b.astype(F32).reshape(1, c),
                _rg_slabs(wa[d], plan, kmax), _rg_slabs(wi[d], plan, kmax),
                ba[d].astype(F32).reshape(1, c), bi[d].astype(F32).reshape(1, c), lam[d].astype(F32).reshape(1, c)]
        if reverse:
            in_specs += [pl.BlockSpec((1, ts, c), lambda bi_, t: (bi_, tmap(t), 0)),
                         pl.BlockSpec((1, ts, c), lambda bi_, t: (bi_, tmap(t), 0))]
            args += [proj, out]
        out = pl.pallas_call(
            functools.partial(_rg_kernel, reverse=reverse, plan=plan, kmax=kmax),
            grid=(b, nt),
            in_specs=in_specs,
            out_specs=pl.BlockSpec((1, ts, c), lambda bi_, t: (bi_, tmap(t), 0)),
            out_shape=jax.ShapeDtypeStruct((b, s, c), BF16 if reverse else F32),
            scratch_shapes=[pltpu.VMEM((ts, c), F32), pltpu.VMEM((ts, c), F32), pltpu.VMEM((SUBLANES, c), F32)],
            compiler_params=_cparams(("parallel", "arbitrary")),
            name="rg_lru_bwd" if reverse else "rg_lru_fwd",
        )(*args)
    return out


TOKB = 512
ROWB = 128
NPASS = (ROWB - 1 + TOKB - 1) // ROWB + 1
ROUTE_GROUP = 4


def _router_kernel(x_ref, g_ref, r_ref, xn_ref, aff_ref, *, ne):
    x = x_ref[...]
    ms = jnp.mean(x * x, axis=-1, keepdims=True)
    xn = x * lax.rsqrt(ms + EPS) * g_ref[...]
    xn_ref[...] = xn.astype(BF16)
    logits = jnp.dot(xn, r_ref[...], precision=lax.Precision.HIGHEST, preferred_element_type=F32)
    lane = lax.broadcasted_iota(jnp.int32, logits.shape, 1)
    logits = jnp.where(lane < ne, logits, -jnp.inf)
    e = jnp.exp(logits - jnp.max(logits, axis=-1, keepdims=True))
    aff_ref[...] = e / jnp.sum(e, axis=-1, keepdims=True)


def router(x, g, router_w, tm=512):
    n, d = x.shape
    ne = router_w.shape[1]
    tm = _pick(n, tm)
    rw = jnp.pad(router_w.astype(F32), ((0, 0), (0, LANES - ne)))
    return pl.pallas_call(
        functools.partial(_router_kernel, ne=ne),
        grid=(n // tm,),
        in_specs=[pl.BlockSpec((tm, d), lambda i: (i, 0)), pl.BlockSpec((1, d), lambda i: (0, 0)),
                  pl.BlockSpec((d, LANES), lambda i: (0, 0))],
        out_specs=[pl.BlockSpec((tm, d), lambda i: (i, 0)), pl.BlockSpec((tm, LANES), lambda i: (i, 0))],
        out_shape=[jax.ShapeDtypeStruct((n, d), BF16), jax.ShapeDtypeStruct((n, LANES), F32)],
        compiler_params=_cparams(("parallel",)),
        name="router",
    )(x, g.reshape(1, d), rw)


def _topk_threshold_kernel(a_ref, thr_ref, need_ref, *, cap, chunk):
    nch = a_ref.shape[0] // chunk

    def count(pred):
        def body(i, acc):
            x = a_ref[pl.ds(pl.multiple_of(i * chunk, chunk), chunk), :]
            return acc + jnp.sum(jnp.where(pred(x), 1.0, 0.0), axis=0, keepdims=True)
        return lax.fori_loop(0, nch, body, jnp.zeros((1, LANES), F32))

    def bit_step(k, v):
        cand = v | lax.shift_left(jnp.int32(1), 30 - k)
        candf = lax.bitcast_convert_type(cand, F32)
        cnt = count(lambda x: x >= candf)
        return jnp.where(cnt >= cap, cand, v)

    v = lax.fori_loop(0, 31, bit_step, jnp.zeros((1, LANES), jnp.int32))
    thr = lax.bitcast_convert_type(v, F32)
    thr_ref[...] = thr
    need_ref[...] = cap - count(lambda x: x > thr)


def topk_threshold(aff, cap):
    n = aff.shape[0]
    vspec = pl.BlockSpec(memory_space=pltpu.VMEM)
    return pl.pallas_call(
        functools.partial(_topk_threshold_kernel, cap=float(cap), chunk=_pick(n, 512)),
        in_specs=[vspec],
        out_specs=[vspec, vspec],
        out_shape=[jax.ShapeDtypeStruct((1, LANES), F32), jax.ShapeDtypeStruct((1, LANES), F32)],
        compiler_params=pltpu.CompilerParams(vmem_limit_bytes=VMEM_LIMIT),
        name="topk_threshold",
    )(aff)


def _rank_kernel(a_ref, thr_ref, need_ref, tri_ref, key_ref, start_ref, ceq, csel):
    @pl.when(pl.program_id(0) == 0)
    def _():
        ceq[...] = jnp.zeros(ceq.shape, F32)
        csel[...] = jnp.zeros(csel.shape, F32)

    x = a_ref[...]
    thr = thr_ref[...]
    gt = jnp.where(x > thr, 1.0, 0.0)
    eq = jnp.where(x == thr, 1.0, 0.0)
    tri = tri_ref[...]
    eq_incl = jnp.dot(tri, eq.astype(BF16), preferred_element_type=F32)
    eq_rank = ceq[...] + eq_incl - eq
    sel = jnp.maximum(gt, eq * jnp.where(eq_rank < need_ref[...], 1.0, 0.0))
    sel_incl = jnp.dot(tri, sel.astype(BF16), preferred_element_type=F32)
    rank = csel[...] + sel_incl - 1.0
    key_ref[...] = jnp.where(sel > 0.0, rank, -1.0).astype(jnp.int32)
    start_ref[0] = jnp.broadcast_to(csel[...], (SUBLANES, LANES)).astype(jnp.int32)
    nrow = x.shape[0]
    ceq[...] += eq_incl[nrow - 1:nrow]
    csel[...] += sel_incl[nrow - 1:nrow]


def token_ranks(aff, thr, need):
    n = aff.shape[0]
    nt = n // TOKB
    tri = jnp.asarray(np.tril(np.ones((TOKB, TOKB), np.float32)), BF16)
    return pl.pallas_call(
        _rank_kernel,
        grid=(nt,),
        in_specs=[pl.BlockSpec((TOKB, LANES), lambda t: (t, 0)), pl.BlockSpec((1, LANES), lambda t: (0, 0)),
                  pl.BlockSpec((1, LANES), lambda t: (0, 0)), pl.BlockSpec((TOKB, TOKB), lambda t: (0, 0))],
        out_specs=[pl.BlockSpec((TOKB, LANES), lambda t: (t, 0)),
                   pl.BlockSpec((1, SUBLANES, LANES), lambda t: (t, 0, 0))],
        out_shape=[jax.ShapeDtypeStruct((n, LANES), jnp.int32),
                   jax.ShapeDtypeStruct((nt, SUBLANES, LANES), jnp.int32)],
        scratch_shapes=[pltpu.VMEM((1, LANES), F32), pltpu.VMEM((1, LANES), F32)],
        compiler_params=_cparams(("arbitrary",)),
        name="token_ranks",
    )(aff, thr, need, tri)


def _route_block(tbl, t, p, e, ne, nblk):
    s0 = tbl[t * ne + e]
    s1 = tbl[(t + 1) * ne + e]
    b0 = s0 // ROWB
    b1 = jnp.maximum(s1 - 1, s0) // ROWB
    return jnp.minimum(jnp.minimum(b0 + p, b1), nblk - 1), (b0 + p) <= b1


def _one_hot(key, e, base):
    col = lax.broadcasted_iota(jnp.int32, (key.shape[0], ROWB), 1)
    ke = jnp.broadcast_to(key[:, e:e + 1], col.shape) - base
    return jnp.where(ke == col, 1.0, 0.0)


def _dispatch_kernel(tbl, xn_ref, key_ref, aff_ref, *rest, ne, nblk):
    xe_refs, gate_refs, last = rest[:ne], rest[ne:2 * ne], rest[2 * ne]
    t, p = pl.program_id(0), pl.program_id(1)

    @pl.when((t == 0) & (p == 0))
    def _():
        for e in range(ne):
            last[e] = -1

    bases, any_active = [], None
    for e in range(ne):
        blk, active = _route_block(tbl, t, p, e, ne, nblk)

        @pl.when(blk != last[e])
        def _():
            xe_refs[e][...] = jnp.zeros(xe_refs[e].shape, BF16)
            gate_refs[e][...] = jnp.zeros(gate_refs[e].shape, F32)

        last[e] = blk
        bases.append(jnp.where(active, blk * ROWB, -(2 ** 30)))
        any_active = active if any_active is None else (any_active | active)

    @pl.when(any_active)
    def _():
        key = key_ref[...]
        xn = xn_ref[...]
        a = aff_ref[...]
        a1 = a.astype(BF16)
        r1 = a - a1.astype(F32)
        a2 = r1.astype(BF16)
        a3 = (r1 - a2.astype(F32)).astype(BF16)
        tn = (((0,), (0,)), ((), ()))
        for g0 in range(0, ne, ROUTE_GROUP):
            grp = range(g0, g0 + ROUTE_GROUP)
            pmat = jnp.concatenate([_one_hot(key, e, bases[e]) for e in grp], axis=1).astype(BF16)
            res = lax.dot_general(pmat, xn, tn, preferred_element_type=F32)
            gres = (lax.dot_general(pmat, a1, tn, preferred_element_type=F32)
                    + lax.dot_general(pmat, a2, tn, preferred_element_type=F32)
                    + lax.dot_general(pmat, a3, tn, preferred_element_type=F32))
            for j, e in enumerate(grp):
                rows = slice(j * ROWB, (j + 1) * ROWB)
                xe_refs[e][...] += res[rows].astype(BF16)
                gate_refs[e][...] += gres[rows]


def dispatch(tbl, xn, key, aff, ne, cap):
    n, d = xn.shape
    nblk = cap // ROWB
    kern = functools.partial(_dispatch_kernel, ne=ne, nblk=nblk)
    tok = lambda t, p, tbl: (t, 0)
    row = lambda e: (lambda t, p, tbl: (_route_block(tbl, t, p, e, ne, nblk)[0], 0))
    outs = pl.pallas_call(
        kern,
        grid_spec=pltpu.PrefetchScalarGridSpec(
            num_scalar_prefetch=1,
            grid=(n // TOKB, NPASS),
            in_specs=[pl.BlockSpec((TOKB, d), tok), pl.BlockSpec((TOKB, LANES), tok),
                      pl.BlockSpec((TOKB, LANES), tok)],
            out_specs=([pl.BlockSpec((ROWB, d), row(e)) for e in range(ne)]
                       + [pl.BlockSpec((ROWB, LANES), row(e)) for e in range(ne)]),
            scratch_shapes=[pltpu.SMEM((ne,), jnp.int32)],
        ),
        out_shape=([jax.ShapeDtypeStruct((cap, d), BF16)] * ne + [jax.ShapeDtypeStruct((cap, LANES), F32)] * ne),
        compiler_params=_cparams(("arbitrary", "arbitrary")),
        name="moe_dispatch",
    )(tbl, xn, key, aff)
    return outs[:ne], outs[ne:]


def _expert_ffn_kernel(x_ref, gate_ref, wg_ref, wu_ref, wd_ref, o_ref, acc_ref, *, lane):
    f = pl.program_id(1)

    @pl.when(f == 0)
    def _():
        acc_ref[...] = jnp.zeros(acc_ref.shape, F32)

    x = x_ref[...]
    g = jnp.dot(x, wg_ref[0], preferred_element_type=F32)
    u = jnp.dot(x, wu_ref[0], preferred_element_type=F32)
    hdn = (g * jax.nn.sigmoid(g) * u).astype(BF16)
    acc_ref[...] += jnp.dot(hdn, wd_ref[0], preferred_element_type=F32)

    @pl.when(f == pl.num_programs(1) - 1)
    def _():
        o_ref[...] = (acc_ref[...] * gate_ref[:, lane:lane + 1]).astype(o_ref.dtype)


def expert_ffn(xe, gate, wg, wu, wd, e, tm=1024, tf=512):
    cap, d = xe.shape
    ff = wg.shape[2]
    tm = _pick(cap, tm)
    tf = _pick(ff, tf)
    return pl.pallas_call(
        functools.partial(_expert_ffn_kernel, lane=e),
        grid=(cap // tm, ff // tf),
        in_specs=[
            pl.BlockSpec((tm, d), lambda i, f: (i, 0)),
            pl.BlockSpec((tm, LANES), lambda i, f: (i, 0)),
            pl.BlockSpec((1, d, tf), lambda i, f: (e, 0, f)),
            pl.BlockSpec((1, d, tf), lambda i, f: (e, 0, f)),
            pl.BlockSpec((1, tf, d), lambda i, f: (e, f, 0)),
        ],
        out_specs=pl.BlockSpec((tm, d), lambda i, f: (i, 0)),
        out_shape=jax.ShapeDtypeStruct((cap, d), BF16),
        scratch_shapes=[pltpu.VMEM((tm, d), F32)],
        compiler_params=_cparams(("parallel", "arbitrary")),
        name="expert_ffn",
    )(xe, gate, wg, wu, wd)


def _combine_kernel(tbl, x_ref, key_ref, *rest, ne, nblk):
    o_refs, y_ref = rest[:ne], rest[ne]
    t, p = pl.program_id(0), pl.program_id(1)

    @pl.when(p == 0)
    def _():
        y_ref[...] = x_ref[...]

    bases, any_active = [], None
    for e in range(ne):
        blk, active = _route_block(tbl, t, p, e, ne, nblk)
        bases.append(jnp.where(active, blk * ROWB, -(2 ** 30)))
        any_active = active if any_active is None else (any_active | active)

    @pl.when(any_active)
    def _():
        key = key_ref[...]
        half = ne // 2
        for g0 in (0, half):
            grp = range(g0, g0 + half)
            pmat = jnp.concatenate([_one_hot(key, e, bases[e]) for e in grp], axis=1).astype(BF16)
            omat = jnp.concatenate([o_refs[e][...] for e in grp], axis=0)
            y_ref[...] += jnp.dot(pmat, omat, preferred_element_type=F32)


def combine(tbl, x, key, outs, cap):
    n, d = x.shape
    ne = len(outs)
    nblk = cap // ROWB
    tok = lambda t, p, tbl: (t, 0)
    row = lambda e: (lambda t, p, tbl: (_route_block(tbl, t, p, e, ne, nblk)[0], 0))
    return pl.pallas_call(
        functools.partial(_combine_kernel, ne=ne, nblk=nblk),
        grid_spec=pltpu.PrefetchScalarGridSpec(
            num_scalar_prefetch=1,
            grid=(n // TOKB, NPASS),
            in_specs=([pl.BlockSpec((TOKB, d), tok), pl.BlockSpec((TOKB, LANES), tok)]
                      + [pl.BlockSpec((ROWB, d), row(e)) for e in range(ne)]),
            out_specs=pl.BlockSpec((TOKB, d), tok),
        ),
        out_shape=jax.ShapeDtypeStruct((n, d), F32),
        compiler_params=_cparams(("arbitrary", "arbitrary")),
        name="moe_combine",
    )(tbl, x, key, *outs)


def ec_moe(x, g, router_w, wg, wu, wd):
    n, d = x.shape
    ne = router_w.shape[1]
    cap = max(1, 2 * n // ne)
    assert n % TOKB == 0 and cap % ROWB == 0 and ne % (2 * ROUTE_GROUP) == 0 and ne <= LANES
    xn, aff = router(x, g, router_w)
    thr, need = topk_threshold(aff, cap)
    key, starts = token_ranks(aff, thr, need)
    tbl = jnp.concatenate([starts[:, 0, :ne], jnp.full((1, ne), cap, jnp.int32)], axis=0).reshape(-1)
    xes, gates = dispatch(tbl, xn, key, aff, ne, cap)
    outs = [expert_ffn(xes[e], gates[e], wg, wu, wd, e) for e in range(ne)]
    return combine(tbl, x, key, outs, cap)


def _trunk(x, p):
    b, s, d = x.shape
    n = b * s
    x = x.reshape(n, d)
    depth = p["norm_mix"].shape[0]
    for layer in range(depth):
        j = layer // 2
        if layer % 2 == 0:
            w_in = p["att_w_in"][j]
            a_qk = (w_in.shape[1] // 7)
            proj = norm_matmul(x, p["norm_mix"][layer], w_in, out_dtype=BF16).reshape(b, s, -1)
            lam_init = 0.8 - 0.6 * float(np.exp(-0.3 * layer))
            qk_a = qk_prep(proj, 0, a_qk, DA_QK_DIM, (p["att_q_norm"][j], p["att_k_norm"][j]),
                           (DA_QK_DIM ** -0.5 * math.log2(math.e), 1.0))
            lamv = jnp.stack([p["att_lam_q1"][j], p["att_lam_k1"][j], p["att_lam_q2"][j],
                              p["att_lam_k2"][j]]).astype(F32)
            out_a = diff_attention(qk_a, proj, 2 * a_qk, lamv, p["att_subln"][j].astype(F32), lam_init)
            qk_b = qk_prep(proj, 3 * a_qk, a_qk, HEAD_DIM, None, (1.0, HEAD_DIM ** -0.5))
            out_b = retention(qk_b, proj, 5 * a_qk, 6 * a_qk, p["ret_norm"][j])
            w_out = p["att_w_out"][j]
            x = matmul_residual([(out_a.reshape(n, -1), w_out[:a_qk]), (out_b.reshape(n, -1), w_out[a_qk:])], x)
        else:
            proj = norm_matmul(x, p["norm_mix"][layer], p["rg_w_in"][j], out_dtype=BF16).reshape(b, s, -1)
            y = rg_lru_block(proj, p["rg_conv_w"][j], p["rg_conv_b"][j], p["rg_wa"][j], p["rg_ba"][j],
                             p["rg_wi"][j], p["rg_bi"][j], p["rg_lambda"][j])
            x = matmul_residual([(y.reshape(n, -1), p["rg_w_out"][j])], x)
        x = ec_moe(x, p["norm_ffn"][layer], p["moe_router"][layer], p["moe_w_gate"][layer],
                   p["moe_w_up"][layer], p["moe_w_down"][layer])
    return x.reshape(b, s, d)


def kernel(x_prompt, x_sample, norm_mix, norm_ffn, att_w_in, att_w_out, att_q_norm, att_k_norm, att_lam_q1, att_lam_k1, att_lam_q2, att_lam_k2, att_subln, ret_norm, rg_w_in, rg_conv_w, rg_conv_b, rg_wa, rg_ba, rg_wi, rg_bi, rg_lambda, rg_w_out, moe_router, moe_w_gate, moe_w_up, moe_w_down):
    p = dict(
        norm_mix=norm_mix.astype(F32), norm_ffn=norm_ffn.astype(F32),
        att_w_in=att_w_in.astype(BF16), att_w_out=att_w_out.astype(BF16),
        att_q_norm=att_q_norm, att_k_norm=att_k_norm,
        att_lam_q1=att_lam_q1, att_lam_k1=att_lam_k1, att_lam_q2=att_lam_q2, att_lam_k2=att_lam_k2,
        att_subln=att_subln, ret_norm=ret_norm,
        rg_w_in=rg_w_in.astype(BF16), rg_conv_w=rg_conv_w, rg_conv_b=rg_conv_b,
        rg_wa=rg_wa, rg_ba=rg_ba, rg_wi=rg_wi, rg_bi=rg_bi, rg_lambda=rg_lambda,
        rg_w_out=rg_w_out.astype(BF16), moe_router=moe_router,
        moe_w_gate=moe_w_gate.astype(BF16), moe_w_up=moe_w_up.astype(BF16), moe_w_down=moe_w_down.astype(BF16),
    )
    return (_trunk(x_prompt, p), _trunk(x_sample, p))
```

```python
import functools
import math

import numpy as np
import jax
import jax.numpy as jnp
from jax import lax
from jax.experimental import pallas as pl
from jax.experimental.pallas import tpu as pltpu

F32 = jnp.float32
BF16 = jnp.bfloat16

HEAD_DIM = 128
DA_QK_DIM = HEAD_DIM // 2
EPS = 1e-6
ROPE_THETA = 10000.0
RG_C = 8.0
CONV_WIDTH = 4
RET_CHUNK = 256
ATTN_ROW_BLOCK = 128
LANES = 128
SUBLANES = 8
HALO = 16
MXU_DIM = 256
VMEM_LIMIT = 52 * 1024 * 1024


def _cparams(sem):
    return pltpu.CompilerParams(dimension_semantics=sem, vmem_limit_bytes=VMEM_LIMIT)


def _pick(n, pref):
    t = min(pref, n)
    while n % t:
        t //= 2
    return t


def _cast_kernel(w_ref, o_ref):
    o_ref[...] = w_ref[0].astype(o_ref.dtype)


def layer_to_bf16(w, layer, tr=256):
    _, ne, r, c = w.shape
    tr = _pick(r, tr)
    return pl.pallas_call(
        _cast_kernel,
        grid=(ne, r // tr),
        in_specs=[pl.BlockSpec((1, 1, tr, c), lambda e, i: (layer, e, i, 0))],
        out_specs=pl.BlockSpec((1, tr, c), lambda e, i: (e, i, 0)),
        out_shape=jax.ShapeDtypeStruct((ne, r, c), BF16),
        compiler_params=_cparams(("parallel", "parallel")),
        name="layer_to_bf16",
    )(w)


def _norm_matmul_kernel(x_ref, g_ref, w_ref, o_ref, xn_ref):
    @pl.when(pl.program_id(1) == 0)
    def _():
        x = x_ref[...]
        ms = jnp.mean(x * x, axis=-1, keepdims=True)
        xn_ref[...] = (x * lax.rsqrt(ms + EPS) * g_ref[...]).astype(BF16)

    o_ref[...] = jnp.dot(xn_ref[...], w_ref[...], preferred_element_type=F32).astype(o_ref.dtype)


def norm_matmul(x, g, w, out_dtype=F32, tm=1024, tn=1024):
    n, d = x.shape
    nout = w.shape[1]
    tm = _pick(n, tm)
    tn = _pick(nout, tn)
    return pl.pallas_call(
        _norm_matmul_kernel,
        grid=(n // tm, nout // tn),
        in_specs=[
            pl.BlockSpec((tm, d), lambda i, j: (i, 0)),
            pl.BlockSpec((1, d), lambda i, j: (0, 0)),
            pl.BlockSpec((d, tn), lambda i, j: (0, j)),
        ],
        out_specs=pl.BlockSpec((tm, tn), lambda i, j: (i, j)),
        out_shape=jax.ShapeDtypeStruct((n, nout), out_dtype),
        scratch_shapes=[pltpu.VMEM((tm, d), BF16)],
        compiler_params=_cparams(("parallel", "arbitrary")),
        name="norm_matmul",
    )(x, g.reshape(1, d), w)


def _matmul_res_kernel(*refs, npairs):
    r_ref, o_ref = refs[2 * npairs], refs[2 * npairs + 1]
    acc = r_ref[...]
    for p in range(npairs):
        acc = acc + jnp.dot(refs[2 * p][...], refs[2 * p + 1][...], preferred_element_type=F32)
    o_ref[...] = acc


def matmul_residual(pairs, res, tm=1024, tn=1024):
    n, dout = res.shape
    tm = _pick(n, tm)
    tn = _pick(dout, tn)
    in_specs, args = [], []
    for y, w in pairs:
        k = y.shape[1]
        in_specs += [pl.BlockSpec((tm, k), lambda i, j: (i, 0)), pl.BlockSpec((k, tn), lambda i, j: (0, j))]
        args += [y, w]
    in_specs.append(pl.BlockSpec((tm, tn), lambda i, j: (i, j)))
    args.append(res)
    return pl.pallas_call(
        functools.partial(_matmul_res_kernel, npairs=len(pairs)),
        grid=(n // tm, dout // tn),
        in_specs=in_specs,
        out_specs=pl.BlockSpec((tm, tn), lambda i, j: (i, j)),
        out_shape=jax.ShapeDtypeStruct((n, dout), F32),
        compiler_params=_cparams(("parallel", "parallel")),
        name="matmul_residual",
    )(*args)


def _rope_tables(seq, group):
    half = group // 2
    inv = ROPE_THETA ** (-jnp.arange(half, dtype=F32) / half)
    ang = jnp.arange(seq, dtype=F32)[:, None] * inv[None, :]
    cos, sin = jnp.cos(ang), jnp.sin(ang)
    reps = LANES // group
    cos_t = jnp.tile(jnp.concatenate([cos, cos], axis=-1), (1, reps))
    sin_t = jnp.tile(jnp.concatenate([-sin, sin], axis=-1), (1, reps))
    return cos_t, sin_t


def _qk_prep_kernel(x_ref, cos_ref, sin_ref, g_ref, gmat_ref, o_ref, *, group, normed, scales):
    j = pl.program_id(2)
    scale = jnp.where(j == 0, scales[0], scales[1]).astype(F32)
    cos = cos_ref[...]
    sin = sin_ref[...]
    g = g_ref[0]
    nh = x_ref.shape[2] // LANES
    lane = lax.broadcasted_iota(jnp.int32, cos.shape, 1)
    first_half = (lane % group) < (group // 2)
    for h in range(nh):
        x = x_ref[0, :, h * LANES:(h + 1) * LANES].astype(F32)
        if normed:
            sq = x * x
            hi = sq.astype(BF16)
            lo = (sq - hi.astype(F32)).astype(BF16)
            ms = (jnp.dot(hi, gmat_ref[...], preferred_element_type=F32)
                  + jnp.dot(lo, gmat_ref[...], preferred_element_type=F32))
            x = x * lax.rsqrt(ms + EPS) * g
        if group == LANES:
            swapped = pltpu.roll(x, LANES // 2, axis=1)
        else:
            fwd = pltpu.roll(x, group // 2, axis=1)
            bwd = pltpu.roll(x, LANES - group // 2, axis=1)
            swapped = jnp.where(first_half, bwd, fwd)
        y = (x * cos + swapped * sin) * scale
        o_ref[0, :, h * LANES:(h + 1) * LANES] = y.astype(o_ref.dtype)


def qk_prep(proj, col0, width, group, norm_w, scales, ts=512):
    b, s, _ = proj.shape
    ts = _pick(s, ts)
    cos_t, sin_t = _rope_tables(s, group)
    normed = norm_w is not None
    if normed:
        g = jnp.stack([jnp.tile(w.astype(F32), LANES // group) for w in norm_w]).reshape(2, 1, LANES)
    else:
        g = jnp.ones((2, 1, LANES), F32)
    grp = np.arange(LANES) // group
    gmat = jnp.asarray((grp[:, None] == grp[None, :]).astype(np.float32) / group, BF16)
    cb = col0 // width
    return pl.pallas_call(
        functools.partial(_qk_prep_kernel, group=group, normed=normed, scales=scales),
        grid=(b, s // ts, 2),
        in_specs=[
            pl.BlockSpec((1, ts, width), lambda bi, si, j: (bi, si, cb + j)),
            pl.BlockSpec((ts, LANES), lambda bi, si, j: (si, 0)),
            pl.BlockSpec((ts, LANES), lambda bi, si, j: (si, 0)),
            pl.BlockSpec((1, 1, LANES), lambda bi, si, j: (j, 0, 0)),
            pl.BlockSpec((LANES, LANES), lambda bi, si, j: (0, 0)),
        ],
        out_specs=pl.BlockSpec((1, ts, width), lambda bi, si, j: (bi, si, j)),
        out_shape=jax.ShapeDtypeStruct((b, s, 2 * width), BF16),
        compiler_params=_cparams(("parallel", "parallel", "parallel")),
        name="qk_prep",
    )(proj, cos_t, sin_t, g, gmat)


def _diff_attn_kernel(q_ref, k_ref, v_ref, lamv_ref, sub_ref, o_ref, m1, a1, m2, a2, *, lam_init):
    ki = pl.program_id(3)

    @pl.when(ki == 0)
    def _():
        for m, a in ((m1, a1), (m2, a2)):
            m[...] = jnp.full(m.shape, -jnp.inf, F32)
            a[...] = jnp.zeros(a.shape, F32)

    q = q_ref[0]
    k = k_ref[0]
    v = v_ref[0].astype(BF16)
    vlane = lax.broadcasted_iota(jnp.int32, v.shape, 1)
    v_ext = jnp.concatenate([v, jnp.where(vlane == 0, 1.0, 0.0).astype(BF16)], axis=1)
    lane = lax.broadcasted_iota(jnp.int32, q.shape, 1)
    zero = jnp.zeros_like(q)
    q_parts = (jnp.where(lane < DA_QK_DIM, q, zero), jnp.where(lane >= DA_QK_DIM, q, zero))
    rb = min(ATTN_ROW_BLOCK, q.shape[0])
    for r0 in range(0, q.shape[0], rb):
        rows = slice(r0, r0 + rb)
        for qq, m, a in ((q_parts[0], m1, a1), (q_parts[1], m2, a2)):
            s = lax.dot_general(qq[rows], k, (((1,), (1,)), ((), ())), preferred_element_type=F32)
            m_old = m[rows]
            m_new = jnp.maximum(m_old, jnp.max(s, axis=-1, keepdims=True))
            p = jnp.exp2(s - m_new).astype(BF16)
            a[rows] = jnp.exp2(m_old - m_new) * a[rows] + jnp.dot(p, v_ext, preferred_element_type=F32)
            m[rows] = m_new

    @pl.when(ki == pl.num_programs(3) - 1)
    def _():
        lv = lamv_ref[...]
        lam = (jnp.exp(jnp.sum(lv[0:1] * lv[1:2], axis=-1, keepdims=True))
               - jnp.exp(jnp.sum(lv[2:3] * lv[3:4], axis=-1, keepdims=True)) + lam_init)
        acc1, acc2 = a1[...], a2[...]
        o = (acc1[:, :HEAD_DIM] / acc1[:, HEAD_DIM:HEAD_DIM + 1]
             - lam * (acc2[:, :HEAD_DIM] / acc2[:, HEAD_DIM:HEAD_DIM + 1]))
        ms = jnp.mean(o * o, axis=-1, keepdims=True)
        y = o * lax.rsqrt(ms + EPS) * sub_ref[0]
        o_ref[0] = (y * (1.0 - lam_init)).astype(o_ref.dtype)


def diff_attention(qk, proj, v_col0, lamv, subln, lam_init, tq=1024, tk=1024):
    b, s, two_a = qk.shape
    nh = two_a // 2 // HEAD_DIM
    tq = _pick(s, tq)
    tk = _pick(s, tk)
    vb = v_col0 // HEAD_DIM
    return pl.pallas_call(
        functools.partial(_diff_attn_kernel, lam_init=lam_init),
        grid=(b, nh, s // tq, s // tk),
        in_specs=[
            pl.BlockSpec((1, tq, HEAD_DIM), lambda bi, h, qi, ki: (bi, qi, h)),
            pl.BlockSpec((1, tk, HEAD_DIM), lambda bi, h, qi, ki: (bi, ki, nh + h)),
            pl.BlockSpec((1, tk, HEAD_DIM), lambda bi, h, qi, ki: (bi, ki, vb + h)),
            pl.BlockSpec((4, DA_QK_DIM), lambda bi, h, qi, ki: (0, 0)),
            pl.BlockSpec((1, 1, HEAD_DIM), lambda bi, h, qi, ki: (h, 0, 0)),
        ],
        out_specs=pl.BlockSpec((1, tq, HEAD_DIM), lambda bi, h, qi, ki: (bi, qi, h)),
        out_shape=jax.ShapeDtypeStruct((b, s, nh * HEAD_DIM), BF16),
        scratch_shapes=[pltpu.VMEM((tq, 1), F32), pltpu.VMEM((tq, 2 * HEAD_DIM), F32),
                        pltpu.VMEM((tq, 1), F32), pltpu.VMEM((tq, 2 * HEAD_DIM), F32)],
        compiler_params=_cparams(("parallel", "parallel", "parallel", "arbitrary")),
        name="diff_attention",
    )(qk, qk, proj, lamv, subln.reshape(nh, 1, HEAD_DIM))


def _retention_consts(nh, reverse):
    c = RET_CHUNK
    log_g = np.log1p(-(2.0 ** (-5.0 - np.arange(nh, dtype=np.float64))))
    pos = np.arange(c, dtype=np.float64)
    diff = pos[:, None] - pos[None, :]
    if reverse:
        mask = diff < 0
        dmat = np.where(mask[None], np.exp(np.where(mask, -diff, 0.0)[None] * log_g[:, None, None]), 0.0)
        qdec = np.exp((c - pos)[None, :] * log_g[:, None])
        kdec = np.exp(pos[None, :] * log_g[:, None])
    else:
        mask = diff >= 0
        dmat = np.where(mask[None], np.exp(np.where(mask, diff, 0.0)[None] * log_g[:, None, None]), 0.0)
        qdec = np.exp((pos + 1.0)[None, :] * log_g[:, None])
        kdec = np.exp((c - 1 - pos)[None, :] * log_g[:, None])
    cdec = np.exp(c * log_g)
    bc = lambda a: jnp.asarray(np.broadcast_to(a[:, :, None], (nh, c, HEAD_DIM)).astype(np.float32))
    return jnp.asarray(dmat.astype(np.float32)), bc(qdec), bc(kdec), [float(x) for x in cdec]


def _retention_kernel(*refs, reverse, cdec, nh):
    if reverse:
        q_ref, k_ref, v_ref, d_ref, qd_ref, kd_ref, f_ref, gb_ref, nw_ref, o_ref, r_ref = refs
    else:
        q_ref, k_ref, v_ref, d_ref, qd_ref, kd_ref, o_ref, r_ref = refs

    @pl.when(pl.program_id(1) == 0)
    def _():
        r_ref[...] = jnp.zeros(r_ref.shape, F32)

    c = RET_CHUNK
    nchunk = q_ref.shape[1] // c
    order = range(nchunk - 1, -1, -1) if reverse else range(nchunk)
    for h in range(nh):
        cols = slice(h * HEAD_DIM, (h + 1) * HEAD_DIM)
        dmat = d_ref[h]
        qd = qd_ref[h]
        kd = kd_ref[h]
        for ci in order:
            rows = slice(ci * c, (ci + 1) * c)
            q = q_ref[0, rows, cols]
            k = k_ref[0, rows, cols]
            v = v_ref[0, rows, cols].astype(F32)
            s = lax.dot_general(q, k, (((1,), (1,)), ((), ())), preferred_element_type=F32) * dmat
            inner = jnp.dot(s.astype(BF16), v.astype(BF16), preferred_element_type=F32)
            r_prev = r_ref[h]
            cross = jnp.dot(q, r_prev.astype(BF16), preferred_element_type=F32) * qd
            kv = lax.dot_general(k, (v * kd).astype(BF16), (((0,), (0,)), ((), ())),
                                 preferred_element_type=F32)
            r_ref[h] = r_prev * cdec[h] + kv
            out = inner + cross
            if reverse:
                ret = out + f_ref[0, rows, cols]
                ms = jnp.mean(ret * ret, axis=-1, keepdims=True)
                y = ret * lax.rsqrt(ms + EPS) * nw_ref[h]
                gate = gb_ref[0, rows, cols].astype(F32)
                y = y * (gate * jax.nn.sigmoid(gate))
                o_ref[0, rows, cols] = y.astype(o_ref.dtype)
            else:
                o_ref[0, rows, cols] = out


def retention(qk, proj, v_col0, g_col0, ret_norm, tr=512):
    b, s, two_w = qk.shape
    w = two_w // 2
    nh = w // HEAD_DIM
    tr = _pick(s, tr)
    nt = s // tr
    vb, gbk = v_col0 // w, g_col0 // w
    outs = None
    for reverse in (False, True):
        dmat, qd, kd, cdec = _retention_consts(nh, reverse)
        tmap = (lambda t: nt - 1 - t) if reverse else (lambda t: t)
        row = lambda cb: (lambda bi, t: (bi, tmap(t), cb))
        const3 = lambda bi, t: (0, 0, 0)
        in_specs = [
            pl.BlockSpec((1, tr, w), row(0)),
            pl.BlockSpec((1, tr, w), row(1)),
            pl.BlockSpec((1, tr, w), row(vb)),
            pl.BlockSpec((nh, RET_CHUNK, RET_CHUNK), const3),
            pl.BlockSpec((nh, RET_CHUNK, HEAD_DIM), const3),
            pl.BlockSpec((nh, RET_CHUNK, HEAD_DIM), const3),
        ]
        args = [qk, qk, proj, dmat, qd, kd]
        if reverse:
            in_specs += [pl.BlockSpec((1, tr, w), row(0)), pl.BlockSpec((1, tr, w), row(gbk)),
                         pl.BlockSpec((nh, 1, HEAD_DIM), const3)]
            args += [outs, proj, ret_norm.astype(F32).reshape(nh, 1, HEAD_DIM)]
        outs = pl.pallas_call(
            functools.partial(_retention_kernel, reverse=reverse, cdec=cdec, nh=nh),
            grid=(b, nt),
            in_specs=in_specs,
            out_specs=pl.BlockSpec((1, tr, w), row(0)),
            out_shape=jax.ShapeDtypeStruct((b, s, w), BF16 if reverse else F32),
            scratch_shapes=[pltpu.VMEM((nh, HEAD_DIM, HEAD_DIM), F32)],
            compiler_params=_cparams(("parallel", "arbitrary")),
            name="retention_bwd" if reverse else "retention_fwd",
        )(*args)
    return outs


def _rg_slab_plan(d_rnn, bs):
    tiles = []
    c0 = 0
    while c0 < d_rnn:
        c1 = min(c0 + MXU_DIM, d_rnn)
        r0 = (c0 // bs) * bs
        r1 = ((c1 - 1) // bs + 1) * bs
        k0 = (r0 // LANES) * LANES
        k1 = min(-(-r1 // LANES) * LANES, d_rnn)
        tiles.append((c0, c1, k0, k1))
        c0 = c1
    kmax = max(k1 - k0 for _, _, k0, k1 in tiles)
    plan = [(c0, c1, min(k0, d_rnn - kmax)) for c0, c1, k0, _ in tiles]
    return plan, kmax


def _rg_slabs(w_blocks, plan, kmax):
    dense = jax.scipy.linalg.block_diag(*[w_blocks[i] for i in range(w_blocks.shape[0])])
    slabs = []
    for c0, c1, k0 in plan:
        sl = dense[k0:k0 + kmax, c0:c1]
        if c1 - c0 < MXU_DIM:
            sl = jnp.pad(sl, ((0, 0), (0, MXU_DIM - (c1 - c0))))
        slabs.append(sl)
    return jnp.stack(slabs).astype(BF16)


def _sigmoid(z):
    return 0.5 * jnp.tanh(0.5 * z) + 0.5


def _rg_kernel(*refs, reverse, plan, kmax):
    if reverse:
        (x_ref, xp_ref, xn_ref, cw_ref, cb_ref, wa_ref, wi_ref, ba_ref, bi_ref, lam_ref,
         g_ref, hf_ref, o_ref, a_s, b_s, carry) = refs
    else:
        (x_ref, xp_ref, xn_ref, cw_ref, cb_ref, wa_ref, wi_ref, ba_ref, bi_ref, lam_ref,
         o_ref, a_s, b_s, carry) = refs
    t = pl.program_id(1)
    nt = pl.num_programs(1)
    tchunk = (nt - 1 - t) if reverse else t

    @pl.when(t == 0)
    def _():
        carry[...] = jnp.zeros(carry.shape, F32)

    ts, c = x_ref.shape[1], x_ref.shape[2]
    xm = x_ref[0].astype(F32)
    xp = jnp.where(tchunk == 0, 0.0, xp_ref[0].astype(F32))
    xn = jnp.where(tchunk == nt - 1, 0.0, xn_ref[0].astype(F32))
    xe = jnp.concatenate([xp, xm, xn], axis=0)
    ext = ts + 2 * HALO
    win = slice(HALO, HALO + ts)
    cw = cw_ref[...]
    xc = (cw[0:1] * pltpu.roll(xe, 2, axis=0)[win] + cw[1:2] * pltpu.roll(xe, 1, axis=0)[win]
          + cw[2:3] * xm + cw[3:4] * pltpu.roll(xe, ext - 1, axis=0)[win] + cb_ref[...])
    xb = xc.astype(BF16)
    lam = lam_ref[...]
    sp = jnp.maximum(-lam, 0.0) + jnp.log1p(jnp.exp(-jnp.abs(lam)))
    for ti, (c0, c1, k0) in enumerate(plan):
        xs = xb[:, k0:k0 + kmax]
        wdt = c1 - c0
        r = _sigmoid(jnp.dot(xs, wa_ref[ti], preferred_element_type=F32)[:, :wdt] + ba_ref[:, c0:c1])
        i = _sigmoid(jnp.dot(xs, wi_ref[ti], preferred_element_type=F32)[:, :wdt] + bi_ref[:, c0:c1])
        log_a = (-RG_C) * sp[:, c0:c1] * r
        a = jnp.exp(log_a)
        a_s[:, c0:c1] = a
        b_s[:, c0:c1] = jnp.sqrt(1.0 - a * a) * (i * xc[:, c0:c1])

    ngroups = ts // SUBLANES
    rowid = lax.broadcasted_iota(jnp.int32, (SUBLANES, c), 0)

    def group(gi, h):
        g = (ngroups - 1 - gi) if reverse else gi
        rows = pl.ds(pl.multiple_of(g * SUBLANES, SUBLANES), SUBLANES)
        a = a_s[rows, :]
        b = b_s[rows, :]
        for k in (1, 2, 4):
            if reverse:
                valid = rowid < SUBLANES - k
                sh = SUBLANES - k
            else:
                valid = rowid >= k
                sh = k
            b = b + a * jnp.where(valid, pltpu.roll(b, sh, axis=0), 0.0)
            a = a * jnp.where(valid, pltpu.roll(a, sh, axis=0), 1.0)
        hg = b + a * h
        b_s[rows, :] = hg
        edge = hg[0:1] if reverse else hg[SUBLANES - 1:SUBLANES]
        return jnp.broadcast_to(edge, (SUBLANES, c))

    carry[...] = lax.fori_loop(0, ngroups, group, carry[...])
    if reverse:
        gate = g_ref[0].astype(F32)
        o_ref[0] = (jax.nn.gelu(gate) * (hf_ref[0] + b_s[...])).astype(o_ref.dtype)
    else:
        o_ref[0] = b_s[...]


def rg_lru_block(proj, conv_w, conv_b, wa, ba, wi, bi, lam, ts=256):
    b, s, two_c = proj.shape
    c = two_c // 2
    nb, bs = wa.shape[1], wa.shape[2]
    ts = _pick(s, ts)
    nt = s // ts
    hb = ts // HALO
    plan, kmax = _rg_slab_plan(c, bs)
    out = None
    for reverse in (False, True):
        d = 1 if reverse else 0
        tmap = (lambda t: nt - 1 - t) if reverse else (lambda t: t)
        const2 = lambda bi_, t: (0, 0)
        const3 = lambda bi_, t: (0, 0, 0)
        in_specs = [
            pl.BlockSpec((1, ts, c), lambda bi_, t: (bi_, tmap(t), 1)),
            pl.BlockSpec((1, HALO, c), lambda bi_, t: (bi_, jnp.maximum(tmap(t) * hb - 1, 0), 1)),
            pl.BlockSpec((1, HALO, c), lambda bi_, t: (bi_, jnp.minimum((tmap(t) + 1) * hb, nt * hb - 1), 1)),
            pl.BlockSpec((CONV_WIDTH, c), const2),
            pl.BlockSpec((1, c), const2),
            pl.BlockSpec((len(plan), kmax, MXU_DIM), const3),
            pl.BlockSpec((len(plan), kmax, MXU_DIM), const3),
            pl.BlockSpec((1, c), const2),
            pl.BlockSpec((1, c), const2),
            pl.BlockSpec((1, c), const2),
        ]
        args = [proj, proj, proj, conv_w.astype(F32), conv_b.astype(F32).reshape(1, c),
                _rg_slabs(wa[d], plan, kmax), _rg_slabs(wi[d], plan, kmax),
                ba[d].astype(F32).reshape(1, c), bi[d].astype(F32).reshape(1, c), lam[d].astype(F32).reshape(1, c)]
        if reverse:
            in_specs += [pl.BlockSpec((1, ts, c), lambda bi_, t: (bi_, tmap(t), 0)),
                         pl.BlockSpec((1, ts, c), lambda bi_, t: (bi_, tmap(t), 0))]
            args += [proj, out]
        out = pl.pallas_call(
            functools.partial(_rg_kernel, reverse=reverse, plan=plan, kmax=kmax),
            grid=(b, nt),
            in_specs=in_specs,
            out_specs=pl.BlockSpec((1, ts, c), lambda bi_, t: (bi_, tmap(t), 0)),
            out_shape=jax.ShapeDtypeStruct((b, s, c), BF16 if reverse else F32),
            scratch_shapes=[pltpu.VMEM((ts, c), F32), pltpu.VMEM((ts, c), F32), pltpu.VMEM((SUBLANES, c), F32)],
            compiler_params=_cparams(("parallel", "arbitrary")),
            name="rg_lru_bwd" if reverse else "rg_lru_fwd",
        )(*args)
    return out


TOKB = 512
ROWB = 128
NPASS = (ROWB - 1 + TOKB - 1) // ROWB + 1
ROUTE_GROUP = 4


def _router_kernel(x_ref, g_ref, r_ref, xn_ref, aff_ref, *, ne):
    x = x_ref[...]
    ms = jnp.mean(x * x, axis=-1, keepdims=True)
    xn = x * lax.rsqrt(ms + EPS) * g_ref[...]
    xn_ref[...] = xn.astype(BF16)
    logits = jnp.dot(xn, r_ref[...], precision=lax.Precision.HIGHEST, preferred_element_type=F32)
    lane = lax.broadcasted_iota(jnp.int32, logits.shape, 1)
    logits = jnp.where(lane < ne, logits, -jnp.inf)
    e = jnp.exp(logits - jnp.max(logits, axis=-1, keepdims=True))
    aff_ref[...] = e / jnp.sum(e, axis=-1, keepdims=True)


def router(x, g, router_w, tm=512):
    n, d = x.shape
    ne = router_w.shape[1]
    tm = _pick(n, tm)
    rw = jnp.pad(router_w.astype(F32), ((0, 0), (0, LANES - ne)))
    return pl.pallas_call(
        functools.partial(_router_kernel, ne=ne),
        grid=(n // tm,),
        in_specs=[pl.BlockSpec((tm, d), lambda i: (i, 0)), pl.BlockSpec((1, d), lambda i: (0, 0)),
                  pl.BlockSpec((d, LANES), lambda i: (0, 0))],
        out_specs=[pl.BlockSpec((tm, d), lambda i: (i, 0)), pl.BlockSpec((tm, LANES), lambda i: (i, 0))],
        out_shape=[jax.ShapeDtypeStruct((n, d), BF16), jax.ShapeDtypeStruct((n, LANES), F32)],
        compiler_params=_cparams(("parallel",)),
        name="router",
    )(x, g.reshape(1, d), rw)


def _topk_threshold_kernel(a_ref, thr_ref, need_ref, *, cap, chunk):
    nch = a_ref.shape[0] // chunk

    def count(pred):
        def body(i, acc):
            x = a_ref[pl.ds(pl.multiple_of(i * chunk, chunk), chunk), :]
            return acc + jnp.sum(jnp.where(pred(x), 1.0, 0.0), axis=0, keepdims=True)
        return lax.fori_loop(0, nch, body, jnp.zeros((1, LANES), F32))

    def bit_step(k, v):
        cand = v | lax.shift_left(jnp.int32(1), 30 - k)
        candf = lax.bitcast_convert_type(cand, F32)
        cnt = count(lambda x: x >= candf)
        return jnp.where(cnt >= cap, cand, v)

    v = lax.fori_loop(0, 31, bit_step, jnp.zeros((1, LANES), jnp.int32))
    thr = lax.bitcast_convert_type(v, F32)
    thr_ref[...] = thr
    need_ref[...] = cap - count(lambda x: x > thr)


def topk_threshold(aff, cap):
    n = aff.shape[0]
    vspec = pl.BlockSpec(memory_space=pltpu.VMEM)
    return pl.pallas_call(
        functools.partial(_topk_threshold_kernel, cap=float(cap), chunk=_pick(n, 512)),
        in_specs=[vspec],
        out_specs=[vspec, vspec],
        out_shape=[jax.ShapeDtypeStruct((1, LANES), F32), jax.ShapeDtypeStruct((1, LANES), F32)],
        compiler_params=pltpu.CompilerParams(vmem_limit_bytes=VMEM_LIMIT),
        name="topk_threshold",
    )(aff)


def _rank_kernel(a_ref, thr_ref, need_ref, tri_ref, key_ref, start_ref, ceq, csel):
    @pl.when(pl.program_id(0) == 0)
    def _():
        ceq[...] = jnp.zeros(ceq.shape, F32)
        csel[...] = jnp.zeros(csel.shape, F32)

    x = a_ref[...]
    thr = thr_ref[...]
    gt = jnp.where(x > thr, 1.0, 0.0)
    eq = jnp.where(x == thr, 1.0, 0.0)
    tri = tri_ref[...]
    eq_incl = jnp.dot(tri, eq.astype(BF16), preferred_element_type=F32)
    eq_rank = ceq[...] + eq_incl - eq
    sel = jnp.maximum(gt, eq * jnp.where(eq_rank < need_ref[...], 1.0, 0.0))
    sel_incl = jnp.dot(tri, sel.astype(BF16), preferred_element_type=F32)
    rank = csel[...] + sel_incl - 1.0
    key_ref[...] = jnp.where(sel > 0.0, rank, -1.0).astype(jnp.int32)
    start_ref[0] = jnp.broadcast_to(csel[...], (SUBLANES, LANES)).astype(jnp.int32)
    nrow = x.shape[0]
    ceq[...] += eq_incl[nrow - 1:nrow]
    csel[...] += sel_incl[nrow - 1:nrow]


def token_ranks(aff, thr, need):
    n = aff.shape[0]
    nt = n // TOKB
    tri = jnp.asarray(np.tril(np.ones((TOKB, TOKB), np.float32)), BF16)
    return pl.pallas_call(
        _rank_kernel,
        grid=(nt,),
        in_specs=[pl.BlockSpec((TOKB, LANES), lambda t: (t, 0)), pl.BlockSpec((1, LANES), lambda t: (0, 0)),
                  pl.BlockSpec((1, LANES), lambda t: (0, 0)), pl.BlockSpec((TOKB, TOKB), lambda t: (0, 0))],
        out_specs=[pl.BlockSpec((TOKB, LANES), lambda t: (t, 0)),
                   pl.BlockSpec((1, SUBLANES, LANES), lambda t: (t, 0, 0))],
        out_shape=[jax.ShapeDtypeStruct((n, LANES), jnp.int32),
                   jax.ShapeDtypeStruct((nt, SUBLANES, LANES), jnp.int32)],
        scratch_shapes=[pltpu.VMEM((1, LANES), F32), pltpu.VMEM((1, LANES), F32)],
        compiler_params=_cparams(("arbitrary",)),
        name="token_ranks",
    )(aff, thr, need, tri)


def _route_tables(starts, cap):
    nblk = cap // ROWB
    s0 = starts[:, None, :]
    s1 = jnp.concatenate([starts[1:], jnp.full_like(starts[:1], cap)], axis=0)[:, None, :]
    b0 = s0 // ROWB
    b1 = jnp.maximum(s1 - 1, s0) // ROWB
    p = jnp.arange(NPASS, dtype=jnp.int32)[None, :, None]
    blk = jnp.minimum(jnp.minimum(b0 + p, b1), nblk - 1)
    active = (b0 + p) <= b1
    base = jnp.where(active, blk * ROWB, -(2 ** 30))
    return (blk.reshape(-1).astype(jnp.int32), base.reshape(-1).astype(jnp.int32),
            jnp.any(active, axis=-1).reshape(-1).astype(jnp.int32))


def _route_entry(t, p, e, ne):
    return (t * NPASS + p) * ne + e


def _one_hot(key, experts, bases):
    per = LANES // ROWB
    lane = lax.broadcasted_iota(jnp.int32, (key.shape[0], LANES), 1)
    col = lane % ROWB
    groups = []
    for g0 in range(0, len(experts), per):
        ke = None
        for j, e in enumerate(experts[g0:g0 + per]):
            kj = jnp.broadcast_to(key[:, e:e + 1], lane.shape) - bases[e]
            ke = kj if ke is None else jnp.where(lane >= j * ROWB, kj, ke)
        groups.append(jnp.where(ke == col, 1.0, 0.0).astype(BF16))
    return groups[0] if len(groups) == 1 else jnp.concatenate(groups, axis=1)


def _dispatch_kernel(blk_tbl, base_tbl, any_tbl, xn_ref, key_ref, aff_ref, *rest, ne):
    xe_refs, gate_refs, last = rest[:ne], rest[ne:2 * ne], rest[2 * ne]
    t, p = pl.program_id(0), pl.program_id(1)

    @pl.when((t == 0) & (p == 0))
    def _():
        for e in range(ne):
            last[e] = -1

    bases = []
    for e in range(ne):
        blk = blk_tbl[_route_entry(t, p, e, ne)]

        @pl.when(blk != last[e])
        def _():
            xe_refs[e][...] = jnp.zeros(xe_refs[e].shape, BF16)
            gate_refs[e][...] = jnp.zeros(gate_refs[e].shape, F32)

        last[e] = blk
        bases.append(base_tbl[_route_entry(t, p, e, ne)])

    @pl.when(any_tbl[t * NPASS + p] != 0)
    def _():
        key = key_ref[...]
        xn = xn_ref[...]
        a = aff_ref[...]
        a1 = a.astype(BF16)
        r1 = a - a1.astype(F32)
        a2 = r1.astype(BF16)
        a3 = (r1 - a2.astype(F32)).astype(BF16)
        tn = (((0,), (0,)), ((), ()))
        for g0 in range(0, ne, ROUTE_GROUP):
            grp = list(range(g0, g0 + ROUTE_GROUP))
            pmat = _one_hot(key, grp, bases)
            res = lax.dot_general(pmat, xn, tn, preferred_element_type=F32)
            gres = (lax.dot_general(pmat, a1, tn, preferred_element_type=F32)
                    + lax.dot_general(pmat, a2, tn, preferred_element_type=F32)
                    + lax.dot_general(pmat, a3, tn, preferred_element_type=F32))
            for j, e in enumerate(grp):
                rows = slice(j * ROWB, (j + 1) * ROWB)
                xe_refs[e][...] += res[rows].astype(BF16)
                gate_refs[e][...] += gres[rows]


def dispatch(tables, xn, key, aff, ne, cap):
    n, d = xn.shape
    kern = functools.partial(_dispatch_kernel, ne=ne)
    tok = lambda t, p, *tbls: (t, 0)
    row = lambda e: (lambda t, p, blk_tbl, *_: (blk_tbl[_route_entry(t, p, e, ne)], 0))
    outs = pl.pallas_call(
        kern,
        grid_spec=pltpu.PrefetchScalarGridSpec(
            num_scalar_prefetch=3,
            grid=(n // TOKB, NPASS),
            in_specs=[pl.BlockSpec((TOKB, d), tok), pl.BlockSpec((TOKB, LANES), tok),
                      pl.BlockSpec((TOKB, LANES), tok)],
            out_specs=([pl.BlockSpec((ROWB, d), row(e)) for e in range(ne)]
                       + [pl.BlockSpec((ROWB, LANES), row(e)) for e in range(ne)]),
            scratch_shapes=[pltpu.SMEM((ne,), jnp.int32)],
        ),
        out_shape=([jax.ShapeDtypeStruct((cap, d), BF16)] * ne + [jax.ShapeDtypeStruct((cap, LANES), F32)] * ne),
        compiler_params=_cparams(("arbitrary", "arbitrary")),
        name="moe_dispatch",
    )(*tables, xn, key, aff)
    return outs[:ne], outs[ne:]


def _expert_ffn_kernel(x_ref, gate_ref, wg_ref, wu_ref, wd_ref, o_ref, acc_ref, *, lane):
    f = pl.program_id(1)

    @pl.when(f == 0)
    def _():
        acc_ref[...] = jnp.zeros(acc_ref.shape, F32)

    x = x_ref[...]
    g = jnp.dot(x, wg_ref[0], preferred_element_type=F32)
    u = jnp.dot(x, wu_ref[0], preferred_element_type=F32)
    hdn = (g * jax.nn.sigmoid(g) * u).astype(BF16)
    acc_ref[...] += jnp.dot(hdn, wd_ref[0], preferred_element_type=F32)

    @pl.when(f == pl.num_programs(1) - 1)
    def _():
        o_ref[...] = (acc_ref[...] * gate_ref[:, lane:lane + 1]).astype(o_ref.dtype)


def expert_ffn(xe, gate, wg, wu, wd, e, tm=1024, tf=512):
    cap, d = xe.shape
    ff = wg.shape[2]
    tm = _pick(cap, tm)
    tf = _pick(ff, tf)
    return pl.pallas_call(
        functools.partial(_expert_ffn_kernel, lane=e),
        grid=(cap // tm, ff // tf),
        in_specs=[
            pl.BlockSpec((tm, d), lambda i, f: (i, 0)),
            pl.BlockSpec((tm, LANES), lambda i, f: (i, 0)),
            pl.BlockSpec((1, d, tf), lambda i, f: (e, 0, f)),
            pl.BlockSpec((1, d, tf), lambda i, f: (e, 0, f)),
            pl.BlockSpec((1, tf, d), lambda i, f: (e, f, 0)),
        ],
        out_specs=pl.BlockSpec((tm, d), lambda i, f: (i, 0)),
        out_shape=jax.ShapeDtypeStruct((cap, d), BF16),
        scratch_shapes=[pltpu.VMEM((tm, d), F32)],
        compiler_params=_cparams(("parallel", "arbitrary")),
        name="expert_ffn",
    )(xe, gate, wg, wu, wd)


def _combine_kernel(blk_tbl, base_tbl, any_tbl, x_ref, key_ref, *rest, ne):
    o_refs, y_ref = rest[:ne], rest[ne]
    t, p = pl.program_id(0), pl.program_id(1)

    @pl.when(p == 0)
    def _():
        y_ref[...] = x_ref[...]

    bases = [base_tbl[_route_entry(t, p, e, ne)] for e in range(ne)]

    @pl.when(any_tbl[t * NPASS + p] != 0)
    def _():
        key = key_ref[...]
        half = ne // 2
        for g0 in (0, half):
            grp = list(range(g0, g0 + half))
            pmat = _one_hot(key, grp, bases)
            omat = jnp.concatenate([o_refs[e][...] for e in grp], axis=0)
            y_ref[...] += jnp.dot(pmat, omat, preferred_element_type=F32)


def combine(tables, x, key, outs):
    n, d = x.shape
    ne = len(outs)
    tok = lambda t, p, *tbls: (t, 0)
    row = lambda e: (lambda t, p, blk_tbl, *_: (blk_tbl[_route_entry(t, p, e, ne)], 0))
    return pl.pallas_call(
        functools.partial(_combine_kernel, ne=ne),
        grid_spec=pltpu.PrefetchScalarGridSpec(
            num_scalar_prefetch=3,
            grid=(n // TOKB, NPASS),
            in_specs=([pl.BlockSpec((TOKB, d), tok), pl.BlockSpec((TOKB, LANES), tok)]
                      + [pl.BlockSpec((ROWB, d), row(e)) for e in range(ne)]),
            out_specs=pl.BlockSpec((TOKB, d), tok),
        ),
        out_shape=jax.ShapeDtypeStruct((n, d), F32),
        compiler_params=_cparams(("arbitrary", "arbitrary")),
        name="moe_combine",
    )(*tables, x, key, *outs)


def ec_moe(x, g, router_w, wg, wu, wd):
    n, d = x.shape
    ne = router_w.shape[1]
    cap = max(1, 2 * n // ne)
    assert n % TOKB == 0 and cap % ROWB == 0 and ne % (2 * ROUTE_GROUP) == 0 and ne <= LANES
    xn, aff = router(x, g, router_w)
    thr, need = topk_threshold(aff, cap)
    key, starts = token_ranks(aff, thr, need)
    tables = _route_tables(starts[:, 0, :ne], cap)
    xes, gates = dispatch(tables, xn, key, aff, ne, cap)
    outs = [expert_ffn(xes[e], gates[e], wg, wu, wd, e) for e in range(ne)]
    return combine(tables, x, key, outs)


def _trunk(x, p):
    b, s, d = x.shape
    n = b * s
    x = x.reshape(n, d)
    depth = p["norm_mix"].shape[0]
    for layer in range(depth):
        j = layer // 2
        if layer % 2 == 0:
            w_in = p["att_w_in"][j]
            a_qk = (w_in.shape[1] // 7)
            proj = norm_matmul(x, p["norm_mix"][layer], w_in, out_dtype=BF16).reshape(b, s, -1)
            lam_init = 0.8 - 0.6 * float(np.exp(-0.3 * layer))
            qk_a = qk_prep(proj, 0, a_qk, DA_QK_DIM, (p["att_q_norm"][j], p["att_k_norm"][j]),
                           (DA_QK_DIM ** -0.5 * math.log2(math.e), 1.0))
            lamv = jnp.stack([p["att_lam_q1"][j], p["att_lam_k1"][j], p["att_lam_q2"][j],
                              p["att_lam_k2"][j]]).astype(F32)
            out_a = diff_attention(qk_a, proj, 2 * a_qk, lamv, p["att_subln"][j].astype(F32), lam_init)
            qk_b = qk_prep(proj, 3 * a_qk, a_qk, HEAD_DIM, None, (1.0, HEAD_DIM ** -0.5))
            out_b = retention(qk_b, proj, 5 * a_qk, 6 * a_qk, p["ret_norm"][j])
            w_out = p["att_w_out"][j]
            x = matmul_residual([(out_a.reshape(n, -1), w_out[:a_qk]), (out_b.reshape(n, -1), w_out[a_qk:])], x)
        else:
            proj = norm_matmul(x, p["norm_mix"][layer], p["rg_w_in"][j], out_dtype=BF16).reshape(b, s, -1)
            y = rg_lru_block(proj, p["rg_conv_w"][j], p["rg_conv_b"][j], p["rg_wa"][j], p["rg_ba"][j],
                             p["rg_wi"][j], p["rg_bi"][j], p["rg_lambda"][j])
            x = matmul_residual([(y.reshape(n, -1), p["rg_w_out"][j])], x)
        x = ec_moe(x, p["norm_ffn"][layer], p["moe_router"][layer], p["moe_w_gate"][layer],
                   p["moe_w_up"][layer], p["moe_w_down"][layer])
    return x.reshape(b, s, d)


def kernel(x_prompt, x_sample, norm_mix, norm_ffn, att_w_in, att_w_out, att_q_norm, att_k_norm, att_lam_q1, att_lam_k1, att_lam_q2, att_lam_k2, att_subln, ret_norm, rg_w_in, rg_conv_w, rg_conv_b, rg_wa, rg_ba, rg_wi, rg_bi, rg_lambda, rg_w_out, moe_router, moe_w_gate, moe_w_up, moe_w_down):
    p = dict(
        norm_mix=norm_mix.astype(F32), norm_ffn=norm_ffn.astype(F32),
        att_w_in=att_w_in.astype(BF16), att_w_out=att_w_out.astype(BF16),
        att_q_norm=att_q_norm, att_k_norm=att_k_norm,
        att_lam_q1=att_lam_q1, att_lam_k1=att_lam_k1, att_lam_q2=att_lam_q2, att_lam_k2=att_lam_k2,
        att_subln=att_subln, ret_norm=ret_norm,
        rg_w_in=rg_w_in.astype(BF16), rg_conv_w=rg_conv_w, rg_conv_b=rg_conv_b,
        rg_wa=rg_wa, rg_ba=rg_ba, rg_wi=rg_wi, rg_bi=rg_bi, rg_lambda=rg_lambda,
        rg_w_out=rg_w_out.astype(BF16), moe_router=moe_router,
        moe_w_gate=[layer_to_bf16(moe_w_gate, l) for l in range(moe_w_gate.shape[0])],
        moe_w_up=[layer_to_bf16(moe_w_up, l) for l in range(moe_w_up.shape[0])],
        moe_w_down=[layer_to_bf16(moe_w_down, l) for l in range(moe_w_down.shape[0])],
    )
    return (_trunk(x_prompt, p), _trunk(x_sample, p))
```

```python
import functools
import math

import numpy as np
import jax
import jax.numpy as jnp
from jax import lax
from jax.experimental import pallas as pl
from jax.experimental.pallas import tpu as pltpu

F32 = jnp.float32
BF16 = jnp.bfloat16

HEAD_DIM = 128
DA_QK_DIM = HEAD_DIM // 2
EPS = 1e-6
ROPE_THETA = 10000.0
RG_C = 8.0
CONV_WIDTH = 4
RET_CHUNK = 256
ATTN_ROW_BLOCK = 128
LANES = 128
SUBLANES = 8
HALO = 16
MXU_DIM = 256
VMEM_LIMIT = 52 * 1024 * 1024


def _cparams(sem):
    return pltpu.CompilerParams(dimension_semantics=sem, vmem_limit_bytes=VMEM_LIMIT)


def _pick(n, pref):
    t = min(pref, n)
    while n % t:
        t //= 2
    return t


def _norm_matmul_kernel(x_ref, g_ref, w_ref, o_ref, xn_ref):
    @pl.when(pl.program_id(1) == 0)
    def _():
        x = x_ref[...]
        ms = jnp.mean(x * x, axis=-1, keepdims=True)
        xn_ref[...] = (x * lax.rsqrt(ms + EPS) * g_ref[...]).astype(BF16)

    o_ref[...] = jnp.dot(xn_ref[...], w_ref[...], preferred_element_type=F32).astype(o_ref.dtype)


def norm_matmul(x, g, w, out_dtype=F32, tm=1024, tn=1024):
    n, d = x.shape
    nout = w.shape[1]
    tm = _pick(n, tm)
    tn = _pick(nout, tn)
    return pl.pallas_call(
        _norm_matmul_kernel,
        grid=(n // tm, nout // tn),
        in_specs=[
            pl.BlockSpec((tm, d), lambda i, j: (i, 0)),
            pl.BlockSpec((1, d), lambda i, j: (0, 0)),
            pl.BlockSpec((d, tn), lambda i, j: (0, j)),
        ],
        out_specs=pl.BlockSpec((tm, tn), lambda i, j: (i, j)),
        out_shape=jax.ShapeDtypeStruct((n, nout), out_dtype),
        scratch_shapes=[pltpu.VMEM((tm, d), BF16)],
        compiler_params=_cparams(("parallel", "arbitrary")),
        name="norm_matmul",
    )(x, g.reshape(1, d), w)


def _matmul_res_kernel(*refs, npairs):
    r_ref, o_ref = refs[2 * npairs], refs[2 * npairs + 1]
    acc = r_ref[...]
    for p in range(npairs):
        acc = acc + jnp.dot(refs[2 * p][...], refs[2 * p + 1][...], preferred_element_type=F32)
    o_ref[...] = acc


def matmul_residual(pairs, res, tm=1024, tn=1024):
    n, dout = res.shape
    tm = _pick(n, tm)
    tn = _pick(dout, tn)
    in_specs, args = [], []
    for y, w in pairs:
        k = y.shape[1]
        in_specs += [pl.BlockSpec((tm, k), lambda i, j: (i, 0)), pl.BlockSpec((k, tn), lambda i, j: (0, j))]
        args += [y, w]
    in_specs.append(pl.BlockSpec((tm, tn), lambda i, j: (i, j)))
    args.append(res)
    return pl.pallas_call(
        functools.partial(_matmul_res_kernel, npairs=len(pairs)),
        grid=(n // tm, dout // tn),
        in_specs=in_specs,
        out_specs=pl.BlockSpec((tm, tn), lambda i, j: (i, j)),
        out_shape=jax.ShapeDtypeStruct((n, dout), F32),
        compiler_params=_cparams(("parallel", "parallel")),
        name="matmul_residual",
    )(*args)


def _rope_tables(seq, group):
    half = group // 2
    inv = ROPE_THETA ** (-jnp.arange(half, dtype=F32) / half)
    ang = jnp.arange(seq, dtype=F32)[:, None] * inv[None, :]
    cos, sin = jnp.cos(ang), jnp.sin(ang)
    reps = LANES // group
    cos_t = jnp.tile(jnp.concatenate([cos, cos], axis=-1), (1, reps))
    sin_t = jnp.tile(jnp.concatenate([-sin, sin], axis=-1), (1, reps))
    return cos_t, sin_t


def _qk_prep_kernel(x_ref, cos_ref, sin_ref, g_ref, gmat_ref, o_ref, *, group, normed, scales):
    j = pl.program_id(2)
    scale = jnp.where(j == 0, scales[0], scales[1]).astype(F32)
    cos = cos_ref[...]
    sin = sin_ref[...]
    g = g_ref[0]
    nh = x_ref.shape[2] // LANES
    lane = lax.broadcasted_iota(jnp.int32, cos.shape, 1)
    first_half = (lane % group) < (group // 2)
    for h in range(nh):
        x = x_ref[0, :, h * LANES:(h + 1) * LANES].astype(F32)
        if normed:
            sq = x * x
            hi = sq.astype(BF16)
            lo = (sq - hi.astype(F32)).astype(BF16)
            ms = (jnp.dot(hi, gmat_ref[...], preferred_element_type=F32)
                  + jnp.dot(lo, gmat_ref[...], preferred_element_type=F32))
            x = x * lax.rsqrt(ms + EPS) * g
        if group == LANES:
            swapped = pltpu.roll(x, LANES // 2, axis=1)
        else:
            fwd = pltpu.roll(x, group // 2, axis=1)
            bwd = pltpu.roll(x, LANES - group // 2, axis=1)
            swapped = jnp.where(first_half, bwd, fwd)
        y = (x * cos + swapped * sin) * scale
        o_ref[0, :, h * LANES:(h + 1) * LANES] = y.astype(o_ref.dtype)


def qk_prep(proj, col0, width, group, norm_w, scales, ts=512):
    b, s, _ = proj.shape
    ts = _pick(s, ts)
    cos_t, sin_t = _rope_tables(s, group)
    normed = norm_w is not None
    if normed:
        g = jnp.stack([jnp.tile(w.astype(F32), LANES // group) for w in norm_w]).reshape(2, 1, LANES)
    else:
        g = jnp.ones((2, 1, LANES), F32)
    grp = np.arange(LANES) // group
    gmat = jnp.asarray((grp[:, None] == grp[None, :]).astype(np.float32) / group, BF16)
    cb = col0 // width
    return pl.pallas_call(
        functools.partial(_qk_prep_kernel, group=group, normed=normed, scales=scales),
        grid=(b, s // ts, 2),
        in_specs=[
            pl.BlockSpec((1, ts, width), lambda bi, si, j: (bi, si, cb + j)),
            pl.BlockSpec((ts, LANES), lambda bi, si, j: (si, 0)),
            pl.BlockSpec((ts, LANES), lambda bi, si, j: (si, 0)),
            pl.BlockSpec((1, 1, LANES), lambda bi, si, j: (j, 0, 0)),
            pl.BlockSpec((LANES, LANES), lambda bi, si, j: (0, 0)),
        ],
        out_specs=pl.BlockSpec((1, ts, width), lambda bi, si, j: (bi, si, j)),
        out_shape=jax.ShapeDtypeStruct((b, s, 2 * width), BF16),
        compiler_params=_cparams(("parallel", "parallel", "parallel")),
        name="qk_prep",
    )(proj, cos_t, sin_t, g, gmat)


def _diff_attn_kernel(q_ref, k_ref, v_ref, lamv_ref, sub_ref, o_ref, m1, a1, m2, a2, *, lam_init):
    ki = pl.program_id(3)

    @pl.when(ki == 0)
    def _():
        for m, a in ((m1, a1), (m2, a2)):
            m[...] = jnp.full(m.shape, -jnp.inf, F32)
            a[...] = jnp.zeros(a.shape, F32)

    q = q_ref[0]
    k = k_ref[0]
    v = v_ref[0].astype(BF16)
    vlane = lax.broadcasted_iota(jnp.int32, v.shape, 1)
    v_ext = jnp.concatenate([v, jnp.where(vlane == 0, 1.0, 0.0).astype(BF16)], axis=1)
    lane = lax.broadcasted_iota(jnp.int32, q.shape, 1)
    zero = jnp.zeros_like(q)
    q_parts = (jnp.where(lane < DA_QK_DIM, q, zero), jnp.where(lane >= DA_QK_DIM, q, zero))
    rb = min(ATTN_ROW_BLOCK, q.shape[0])
    for r0 in range(0, q.shape[0], rb):
        rows = slice(r0, r0 + rb)
        for qq, m, a in ((q_parts[0], m1, a1), (q_parts[1], m2, a2)):
            s = lax.dot_general(qq[rows], k, (((1,), (1,)), ((), ())), preferred_element_type=F32)
            m_old = m[rows]
            m_new = jnp.maximum(m_old, jnp.max(s, axis=-1, keepdims=True))
            p = jnp.exp2(s - m_new).astype(BF16)
            a[rows] = jnp.exp2(m_old - m_new) * a[rows] + jnp.dot(p, v_ext, preferred_element_type=F32)
            m[rows] = m_new

    @pl.when(ki == pl.num_programs(3) - 1)
    def _():
        lv = lamv_ref[...]
        lam = (jnp.exp(jnp.sum(lv[0:1] * lv[1:2], axis=-1, keepdims=True))
               - jnp.exp(jnp.sum(lv[2:3] * lv[3:4], axis=-1, keepdims=True)) + lam_init)
        acc1, acc2 = a1[...], a2[...]
        o = (acc1[:, :HEAD_DIM] / acc1[:, HEAD_DIM:HEAD_DIM + 1]
             - lam * (acc2[:, :HEAD_DIM] / acc2[:, HEAD_DIM:HEAD_DIM + 1]))
        ms = jnp.mean(o * o, axis=-1, keepdims=True)
        y = o * lax.rsqrt(ms + EPS) * sub_ref[0]
        o_ref[0] = (y * (1.0 - lam_init)).astype(o_ref.dtype)


def diff_attention(qk, proj, v_col0, lamv, subln, lam_init, tq=1024, tk=1024):
    b, s, two_a = qk.shape
    nh = two_a // 2 // HEAD_DIM
    tq = _pick(s, tq)
    tk = _pick(s, tk)
    vb = v_col0 // HEAD_DIM
    return pl.pallas_call(
        functools.partial(_diff_attn_kernel, lam_init=lam_init),
        grid=(b, nh, s // tq, s // tk),
        in_specs=[
            pl.BlockSpec((1, tq, HEAD_DIM), lambda bi, h, qi, ki: (bi, qi, h)),
            pl.BlockSpec((1, tk, HEAD_DIM), lambda bi, h, qi, ki: (bi, ki, nh + h)),
            pl.BlockSpec((1, tk, HEAD_DIM), lambda bi, h, qi, ki: (bi, ki, vb + h)),
            pl.BlockSpec((4, DA_QK_DIM), lambda bi, h, qi, ki: (0, 0)),
            pl.BlockSpec((1, 1, HEAD_DIM), lambda bi, h, qi, ki: (h, 0, 0)),
        ],
        out_specs=pl.BlockSpec((1, tq, HEAD_DIM), lambda bi, h, qi, ki: (bi, qi, h)),
        out_shape=jax.ShapeDtypeStruct((b, s, nh * HEAD_DIM), BF16),
        scratch_shapes=[pltpu.VMEM((tq, 1), F32), pltpu.VMEM((tq, 2 * HEAD_DIM), F32),
                        pltpu.VMEM((tq, 1), F32), pltpu.VMEM((tq, 2 * HEAD_DIM), F32)],
        compiler_params=_cparams(("parallel", "parallel", "parallel", "arbitrary")),
        name="diff_attention",
    )(qk, qk, proj, lamv, subln.reshape(nh, 1, HEAD_DIM))


def _retention_consts(nh, reverse):
    c = RET_CHUNK
    log_g = np.log1p(-(2.0 ** (-5.0 - np.arange(nh, dtype=np.float64))))
    pos = np.arange(c, dtype=np.float64)
    diff = pos[:, None] - pos[None, :]
    if reverse:
        mask = diff < 0
        dmat = np.where(mask[None], np.exp(np.where(mask, -diff, 0.0)[None] * log_g[:, None, None]), 0.0)
        qdec = np.exp((c - pos)[None, :] * log_g[:, None])
        kdec = np.exp(pos[None, :] * log_g[:, None])
    else:
        mask = diff >= 0
        dmat = np.where(mask[None], np.exp(np.where(mask, diff, 0.0)[None] * log_g[:, None, None]), 0.0)
        qdec = np.exp((pos + 1.0)[None, :] * log_g[:, None])
        kdec = np.exp((c - 1 - pos)[None, :] * log_g[:, None])
    cdec = np.exp(c * log_g)
    bc = lambda a: jnp.asarray(np.broadcast_to(a[:, :, None], (nh, c, HEAD_DIM)).astype(np.float32))
    return jnp.asarray(dmat.astype(np.float32)), bc(qdec), bc(kdec), [float(x) for x in cdec]


def _retention_kernel(*refs, reverse, cdec, nh):
    if reverse:
        q_ref, k_ref, v_ref, d_ref, qd_ref, kd_ref, f_ref, gb_ref, nw_ref, o_ref, r_ref = refs
    else:
        q_ref, k_ref, v_ref, d_ref, qd_ref, kd_ref, o_ref, r_ref = refs

    @pl.when(pl.program_id(1) == 0)
    def _():
        r_ref[...] = jnp.zeros(r_ref.shape, F32)

    c = RET_CHUNK
    nchunk = q_ref.shape[1] // c
    order = range(nchunk - 1, -1, -1) if reverse else range(nchunk)
    for h in range(nh):
        cols = slice(h * HEAD_DIM, (h + 1) * HEAD_DIM)
        dmat = d_ref[h]
        qd = qd_ref[h]
        kd = kd_ref[h]
        for ci in order:
            rows = slice(ci * c, (ci + 1) * c)
            q = q_ref[0, rows, cols]
            k = k_ref[0, rows, cols]
            v = v_ref[0, rows, cols].astype(F32)
            s = lax.dot_general(q, k, (((1,), (1,)), ((), ())), preferred_element_type=F32) * dmat
            inner = jnp.dot(s.astype(BF16), v.astype(BF16), preferred_element_type=F32)
            r_prev = r_ref[h]
            cross = jnp.dot(q, r_prev.astype(BF16), preferred_element_type=F32) * qd
            kv = lax.dot_general(k, (v * kd).astype(BF16), (((0,), (0,)), ((), ())),
                                 preferred_element_type=F32)
            r_ref[h] = r_prev * cdec[h] + kv
            out = inner + cross
            if reverse:
                ret = out + f_ref[0, rows, cols]
                ms = jnp.mean(ret * ret, axis=-1, keepdims=True)
                y = ret * lax.rsqrt(ms + EPS) * nw_ref[h]
                gate = gb_ref[0, rows, cols].astype(F32)
                y = y * (gate * jax.nn.sigmoid(gate))
                o_ref[0, rows, cols] = y.astype(o_ref.dtype)
            else:
                o_ref[0, rows, cols] = out


def retention(qk, proj, v_col0, g_col0, ret_norm, tr=512):
    b, s, two_w = qk.shape
    w = two_w // 2
    nh = w // HEAD_DIM
    tr = _pick(s, tr)
    nt = s // tr
    vb, gbk = v_col0 // w, g_col0 // w
    outs = None
    for reverse in (False, True):
        dmat, qd, kd, cdec = _retention_consts(nh, reverse)
        tmap = (lambda t: nt - 1 - t) if reverse else (lambda t: t)
        row = lambda cb: (lambda bi, t: (bi, tmap(t), cb))
        const3 = lambda bi, t: (0, 0, 0)
        in_specs = [
            pl.BlockSpec((1, tr, w), row(0)),
            pl.BlockSpec((1, tr, w), row(1)),
            pl.BlockSpec((1, tr, w), row(vb)),
            pl.BlockSpec((nh, RET_CHUNK, RET_CHUNK), const3),
            pl.BlockSpec((nh, RET_CHUNK, HEAD_DIM), const3),
            pl.BlockSpec((nh, RET_CHUNK, HEAD_DIM), const3),
        ]
        args = [qk, qk, proj, dmat, qd, kd]
        if reverse:
            in_specs += [pl.BlockSpec((1, tr, w), row(0)), pl.BlockSpec((1, tr, w), row(gbk)),
                         pl.BlockSpec((nh, 1, HEAD_DIM), const3)]
            args += [outs, proj, ret_norm.astype(F32).reshape(nh, 1, HEAD_DIM)]
        outs = pl.pallas_call(
            functools.partial(_retention_kernel, reverse=reverse, cdec=cdec, nh=nh),
            grid=(b, nt),
            in_specs=in_specs,
            out_specs=pl.BlockSpec((1, tr, w), row(0)),
            out_shape=jax.ShapeDtypeStruct((b, s, w), BF16 if reverse else F32),
            scratch_shapes=[pltpu.VMEM((nh, HEAD_DIM, HEAD_DIM), F32)],
            compiler_params=_cparams(("parallel", "arbitrary")),
            name="retention_bwd" if reverse else "retention_fwd",
        )(*args)
    return outs


def _rg_slab_plan(d_rnn, bs):
    tiles = []
    c0 = 0
    while c0 < d_rnn:
        c1 = min(c0 + MXU_DIM, d_rnn)
        r0 = (c0 // bs) * bs
        r1 = ((c1 - 1) // bs + 1) * bs
        k0 = (r0 // LANES) * LANES
        k1 = min(-(-r1 // LANES) * LANES, d_rnn)
        tiles.append((c0, c1, k0, k1))
        c0 = c1
    kmax = max(k1 - k0 for _, _, k0, k1 in tiles)
    plan = [(c0, c1, min(k0, d_rnn - kmax)) for c0, c1, k0, _ in tiles]
    return plan, kmax


def _rg_slabs(w_blocks, plan, kmax):
    dense = jax.scipy.linalg.block_diag(*[w_blocks[i] for i in range(w_blocks.shape[0])])
    slabs = []
    for c0, c1, k0 in plan:
        sl = dense[k0:k0 + kmax, c0:c1]
        if c1 - c0 < MXU_DIM:
            sl = jnp.pad(sl, ((0, 0), (0, MXU_DIM - (c1 - c0))))
        slabs.append(sl)
    return jnp.stack(slabs).astype(BF16)


def _sigmoid(z):
    return 0.5 * jnp.tanh(0.5 * z) + 0.5


def _rg_kernel(*refs, reverse, plan, kmax):
    if reverse:
        (x_ref, xp_ref, xn_ref, cw_ref, cb_ref, wa_ref, wi_ref, ba_ref, bi_ref, lam_ref,
         g_ref, hf_ref, o_ref, a_s, b_s, carry) = refs
    else:
        (x_ref, xp_ref, xn_ref, cw_ref, cb_ref, wa_ref, wi_ref, ba_ref, bi_ref, lam_ref,
         o_ref, a_s, b_s, carry) = refs
    t = pl.program_id(1)
    nt = pl.num_programs(1)
    tchunk = (nt - 1 - t) if reverse else t

    @pl.when(t == 0)
    def _():
        carry[...] = jnp.zeros(carry.shape, F32)

    ts, c = x_ref.shape[1], x_ref.shape[2]
    xm = x_ref[0].astype(F32)
    xp = jnp.where(tchunk == 0, 0.0, xp_ref[0].astype(F32))
    xn = jnp.where(tchunk == nt - 1, 0.0, xn_ref[0].astype(F32))
    xe = jnp.concatenate([xp, xm, xn], axis=0)
    ext = ts + 2 * HALO
    win = slice(HALO, HALO + ts)
    cw = cw_ref[...]
    xc = (cw[0:1] * pltpu.roll(xe, 2, axis=0)[win] + cw[1:2] * pltpu.roll(xe, 1, axis=0)[win]
          + cw[2:3] * xm + cw[3:4] * pltpu.roll(xe, ext - 1, axis=0)[win] + cb_ref[...])
    xb = xc.astype(BF16)
    lam = lam_ref[...]
    sp = jnp.maximum(-lam, 0.0) + jnp.log1p(jnp.exp(-jnp.abs(lam)))
    for ti, (c0, c1, k0) in enumerate(plan):
        xs = xb[:, k0:k0 + kmax]
        wdt = c1 - c0
        r = _sigmoid(jnp.dot(xs, wa_ref[ti], preferred_element_type=F32)[:, :wdt] + ba_ref[:, c0:c1])
        i = _sigmoid(jnp.dot(xs, wi_ref[ti], preferred_element_type=F32)[:, :wdt] + bi_ref[:, c0:c1])
        log_a = (-RG_C) * sp[:, c0:c1] * r
        a = jnp.exp(log_a)
        a_s[:, c0:c1] = a
        b_s[:, c0:c1] = jnp.sqrt(1.0 - a * a) * (i * xc[:, c0:c1])

    ngroups = ts // SUBLANES
    rowid = lax.broadcasted_iota(jnp.int32, (SUBLANES, c), 0)

    def group(gi, h):
        g = (ngroups - 1 - gi) if reverse else gi
        rows = pl.ds(pl.multiple_of(g * SUBLANES, SUBLANES), SUBLANES)
        a = a_s[rows, :]
        b = b_s[rows, :]
        for k in (1, 2, 4):
            if reverse:
                valid = rowid < SUBLANES - k
                sh = SUBLANES - k
            else:
                valid = rowid >= k
                sh = k
            b = b + a * jnp.where(valid, pltpu.roll(b, sh, axis=0), 0.0)
            a = a * jnp.where(valid, pltpu.roll(a, sh, axis=0), 1.0)
        hg = b + a * h
        b_s[rows, :] = hg
        edge = hg[0:1] if reverse else hg[SUBLANES - 1:SUBLANES]
        return jnp.broadcast_to(edge, (SUBLANES, c))

    carry[...] = lax.fori_loop(0, ngroups, group, carry[...])
    if reverse:
        gate = g_ref[0].astype(F32)
        o_ref[0] = (jax.nn.gelu(gate) * (hf_ref[0] + b_s[...])).astype(o_ref.dtype)
    else:
        o_ref[0] = b_s[...]


def rg_lru_block(proj, conv_w, conv_b, wa, ba, wi, bi, lam, ts=256):
    b, s, two_c = proj.shape
    c = two_c // 2
    nb, bs = wa.shape[1], wa.shape[2]
    ts = _pick(s, ts)
    nt = s // ts
    hb = ts // HALO
    plan, kmax = _rg_slab_plan(c, bs)
    out = None
    for reverse in (False, True):
        d = 1 if reverse else 0
        tmap = (lambda t: nt - 1 - t) if reverse else (lambda t: t)
        const2 = lambda bi_, t: (0, 0)
        const3 = lambda bi_, t: (0, 0, 0)
        in_specs = [
            pl.BlockSpec((1, ts, c), lambda bi_, t: (bi_, tmap(t), 1)),
            pl.BlockSpec((1, HALO, c), lambda bi_, t: (bi_, jnp.maximum(tmap(t) * hb - 1, 0), 1)),
            pl.BlockSpec((1, HALO, c), lambda bi_, t: (bi_, jnp.minimum((tmap(t) + 1) * hb, nt * hb - 1), 1)),
            pl.BlockSpec((CONV_WIDTH, c), const2),
            pl.BlockSpec((1, c), const2),
            pl.BlockSpec((len(plan), kmax, MXU_DIM), const3),
            pl.BlockSpec((len(plan), kmax, MXU_DIM), const3),
            pl.BlockSpec((1, c), const2),
            pl.BlockSpec((1, c), const2),
            pl.BlockSpec((1, c), const2),
        ]
        args = [proj, proj, proj, conv_w.astype(F32), conv_b.astype(F32).reshape(1, c),
                _rg_slabs(wa[d], plan, kmax), _rg_slabs(wi[d], plan, kmax),
                ba[d].astype(F32).reshape(1, c), bi[d].astype(F32).reshape(1, c), lam[d].astype(F32).reshape(1, c)]
        if reverse:
            in_specs += [pl.BlockSpec((1, ts, c), lambda bi_, t: (bi_, tmap(t), 0)),
                         pl.BlockSpec((1, ts, c), lambda bi_, t: (bi_, tmap(t), 0))]
            args += [proj, out]
        out = pl.pallas_call(
            functools.partial(_rg_kernel, reverse=reverse, plan=plan, kmax=kmax),
            grid=(b, nt),
            in_specs=in_specs,
            out_specs=pl.BlockSpec((1, ts, c), lambda bi_, t: (bi_, tmap(t), 0)),
            out_shape=jax.ShapeDtypeStruct((b, s, c), BF16 if reverse else F32),
            scratch_shapes=[pltpu.VMEM((ts, c), F32), pltpu.VMEM((ts, c), F32), pltpu.VMEM((SUBLANES, c), F32)],
            compiler_params=_cparams(("parallel", "arbitrary")),
            name="rg_lru_bwd" if reverse else "rg_lru_fwd",
        )(*args)
    return out


TOKB = 512
ROWB = 128
NPASS = (ROWB - 1 + TOKB - 1) // ROWB + 1
ROUTE_GROUP = 4
COMBINE_GROUP = 8


def _router_kernel(x_ref, g_ref, r_ref, xn_ref, aff_ref, *, ne):
    x = x_ref[...]
    ms = jnp.mean(x * x, axis=-1, keepdims=True)
    xn = x * lax.rsqrt(ms + EPS) * g_ref[...]
    xn_ref[...] = xn.astype(BF16)
    logits = jnp.dot(xn, r_ref[...], precision=lax.Precision.HIGHEST, preferred_element_type=F32)
    lane = lax.broadcasted_iota(jnp.int32, logits.shape, 1)
    logits = jnp.where(lane < ne, logits, -jnp.inf)
    e = jnp.exp(logits - jnp.max(logits, axis=-1, keepdims=True))
    aff_ref[...] = e / jnp.sum(e, axis=-1, keepdims=True)


def router(x, g, router_w, tm=512):
    n, d = x.shape
    ne = router_w.shape[1]
    tm = _pick(n, tm)
    rw = jnp.pad(router_w.astype(F32), ((0, 0), (0, LANES - ne)))
    return pl.pallas_call(
        functools.partial(_router_kernel, ne=ne),
        grid=(n // tm,),
        in_specs=[pl.BlockSpec((tm, d), lambda i: (i, 0)), pl.BlockSpec((1, d), lambda i: (0, 0)),
                  pl.BlockSpec((d, LANES), lambda i: (0, 0))],
        out_specs=[pl.BlockSpec((tm, d), lambda i: (i, 0)), pl.BlockSpec((tm, LANES), lambda i: (i, 0))],
        out_shape=[jax.ShapeDtypeStruct((n, d), BF16), jax.ShapeDtypeStruct((n, LANES), F32)],
        compiler_params=_cparams(("parallel",)),
        name="router",
    )(x, g.reshape(1, d), rw)


def _topk_threshold_kernel(a_ref, thr_ref, need_ref, *, cap, chunk):
    nch = a_ref.shape[0] // chunk

    def count(pred):
        def body(i, acc):
            x = a_ref[pl.ds(pl.multiple_of(i * chunk, chunk), chunk), :]
            return acc + jnp.sum(jnp.where(pred(x), 1.0, 0.0), axis=0, keepdims=True)
        return lax.fori_loop(0, nch, body, jnp.zeros((1, LANES), F32))

    def bit_step(k, v):
        cand = v | lax.shift_left(jnp.int32(1), 30 - k)
        candf = lax.bitcast_convert_type(cand, F32)
        cnt = count(lambda x: x >= candf)
        return jnp.where(cnt >= cap, cand, v)

    v = lax.fori_loop(0, 31, bit_step, jnp.zeros((1, LANES), jnp.int32))
    thr = lax.bitcast_convert_type(v, F32)
    thr_ref[...] = thr
    need_ref[...] = cap - count(lambda x: x > thr)


def topk_threshold(aff, cap):
    n = aff.shape[0]
    vspec = pl.BlockSpec(memory_space=pltpu.VMEM)
    return pl.pallas_call(
        functools.partial(_topk_threshold_kernel, cap=float(cap), chunk=_pick(n, 512)),
        in_specs=[vspec],
        out_specs=[vspec, vspec],
        out_shape=[jax.ShapeDtypeStruct((1, LANES), F32), jax.ShapeDtypeStruct((1, LANES), F32)],
        compiler_params=pltpu.CompilerParams(vmem_limit_bytes=VMEM_LIMIT),
        name="topk_threshold",
    )(aff)


def _rank_kernel(a_ref, thr_ref, need_ref, tri_ref, key_ref, start_ref, ceq, csel):
    @pl.when(pl.program_id(0) == 0)
    def _():
        ceq[...] = jnp.zeros(ceq.shape, F32)
        csel[...] = jnp.zeros(csel.shape, F32)

    x = a_ref[...]
    thr = thr_ref[...]
    gt = jnp.where(x > thr, 1.0, 0.0)
    eq = jnp.where(x == thr, 1.0, 0.0)
    tri = tri_ref[...]
    eq_incl = jnp.dot(tri, eq.astype(BF16), preferred_element_type=F32)
    eq_rank = ceq[...] + eq_incl - eq
    sel = jnp.maximum(gt, eq * jnp.where(eq_rank < need_ref[...], 1.0, 0.0))
    sel_incl = jnp.dot(tri, sel.astype(BF16), preferred_element_type=F32)
    rank = csel[...] + sel_incl - 1.0
    key_ref[...] = jnp.where(sel > 0.0, rank, -1.0).astype(jnp.int32)
    start_ref[0] = jnp.broadcast_to(csel[...], (SUBLANES, LANES)).astype(jnp.int32)
    nrow = x.shape[0]
    ceq[...] += eq_incl[nrow - 1:nrow]
    csel[...] += sel_incl[nrow - 1:nrow]


def token_ranks(aff, thr, need):
    n = aff.shape[0]
    nt = n // TOKB
    tri = jnp.asarray(np.tril(np.ones((TOKB, TOKB), np.float32)), BF16)
    return pl.pallas_call(
        _rank_kernel,
        grid=(nt,),
        in_specs=[pl.BlockSpec((TOKB, LANES), lambda t: (t, 0)), pl.BlockSpec((1, LANES), lambda t: (0, 0)),
                  pl.BlockSpec((1, LANES), lambda t: (0, 0)), pl.BlockSpec((TOKB, TOKB), lambda t: (0, 0))],
        out_specs=[pl.BlockSpec((TOKB, LANES), lambda t: (t, 0)),
                   pl.BlockSpec((1, SUBLANES, LANES), lambda t: (t, 0, 0))],
        out_shape=[jax.ShapeDtypeStruct((n, LANES), jnp.int32),
                   jax.ShapeDtypeStruct((nt, SUBLANES, LANES), jnp.int32)],
        scratch_shapes=[pltpu.VMEM((1, LANES), F32), pltpu.VMEM((1, LANES), F32)],
        compiler_params=_cparams(("arbitrary",)),
        name="token_ranks",
    )(aff, thr, need, tri)


def _route_tables(starts, cap):
    nblk = cap // ROWB
    s0 = starts[:, None, :]
    s1 = jnp.concatenate([starts[1:], jnp.full_like(starts[:1], cap)], axis=0)[:, None, :]
    b0 = s0 // ROWB
    b1 = jnp.maximum(s1 - 1, s0) // ROWB
    p = jnp.arange(NPASS, dtype=jnp.int32)[None, :, None]
    blk = jnp.minimum(jnp.minimum(b0 + p, b1), nblk - 1)
    active = (b0 + p) <= b1
    base = jnp.where(active, blk * ROWB, -(2 ** 30))
    return blk.reshape(-1).astype(jnp.int32), base.reshape(-1).astype(jnp.int32)


def _route_entry(t, p, e, ne):
    return (t * NPASS + p) * ne + e


def _one_hot(key, experts, bases):
    per = LANES // ROWB
    lane = lax.broadcasted_iota(jnp.int32, (key.shape[0], LANES), 1)
    col = lane % ROWB
    groups = []
    for g0 in range(0, len(experts), per):
        ke = None
        for j, e in enumerate(experts[g0:g0 + per]):
            kj = jnp.broadcast_to(key[:, e:e + 1], lane.shape) - bases[e]
            ke = kj if ke is None else jnp.where(lane >= j * ROWB, kj, ke)
        groups.append(jnp.where(ke == col, 1.0, 0.0).astype(BF16))
    return groups[0] if len(groups) == 1 else jnp.concatenate(groups, axis=1)


def _any_active(bases, experts):
    flag = bases[experts[0]] >= 0
    for e in experts[1:]:
        flag = flag | (bases[e] >= 0)
    return flag


def _dispatch_kernel(blk_tbl, base_tbl, xn_ref, key_ref, aff_ref, *rest, ne):
    xe_refs, gate_refs, last = rest[:ne], rest[ne:2 * ne], rest[2 * ne]
    t, p = pl.program_id(0), pl.program_id(1)

    @pl.when((t == 0) & (p == 0))
    def _():
        for e in range(ne):
            last[e] = -1

    bases = []
    for e in range(ne):
        blk = blk_tbl[_route_entry(t, p, e, ne)]

        @pl.when(blk != last[e])
        def _():
            xe_refs[e][...] = jnp.zeros(xe_refs[e].shape, BF16)
            gate_refs[e][...] = jnp.zeros(gate_refs[e].shape, F32)

        last[e] = blk
        bases.append(base_tbl[_route_entry(t, p, e, ne)])

    @pl.when(_any_active(bases, list(range(ne))))
    def _():
        key = key_ref[...]
        xn = xn_ref[...]
        a = aff_ref[...]
        a1 = a.astype(BF16)
        r1 = a - a1.astype(F32)
        a2 = r1.astype(BF16)
        a3 = (r1 - a2.astype(F32)).astype(BF16)
        tn = (((0,), (0,)), ((), ()))
        for g0 in range(0, ne, ROUTE_GROUP):
            grp = list(range(g0, g0 + ROUTE_GROUP))
            pmat = _one_hot(key, grp, bases)
            res = lax.dot_general(pmat, xn, tn, preferred_element_type=F32)
            gres = (lax.dot_general(pmat, a1, tn, preferred_element_type=F32)
                    + lax.dot_general(pmat, a2, tn, preferred_element_type=F32)
                    + lax.dot_general(pmat, a3, tn, preferred_element_type=F32))
            for j, e in enumerate(grp):
                rows = slice(j * ROWB, (j + 1) * ROWB)
                xe_refs[e][...] += res[rows].astype(BF16)
                gate_refs[e][...] += gres[rows]


def dispatch(tables, xn, key, aff, ne, cap):
    n, d = xn.shape
    kern = functools.partial(_dispatch_kernel, ne=ne)
    tok = lambda t, p, *tbls: (t, 0)
    row = lambda e: (lambda t, p, blk_tbl, *_: (blk_tbl[_route_entry(t, p, e, ne)], 0))
    outs = pl.pallas_call(
        kern,
        grid_spec=pltpu.PrefetchScalarGridSpec(
            num_scalar_prefetch=2,
            grid=(n // TOKB, NPASS),
            in_specs=[pl.BlockSpec((TOKB, d), tok), pl.BlockSpec((TOKB, LANES), tok),
                      pl.BlockSpec((TOKB, LANES), tok)],
            out_specs=([pl.BlockSpec((ROWB, d), row(e)) for e in range(ne)]
                       + [pl.BlockSpec((ROWB, LANES), row(e)) for e in range(ne)]),
            scratch_shapes=[pltpu.SMEM((ne,), jnp.int32)],
        ),
        out_shape=([jax.ShapeDtypeStruct((cap, d), BF16)] * ne + [jax.ShapeDtypeStruct((cap, LANES), F32)] * ne),
        compiler_params=_cparams(("arbitrary", "arbitrary")),
        name="moe_dispatch",
    )(*tables, xn, key, aff)
    return outs[:ne], outs[ne:]


def _expert_ffn_kernel(x_ref, gate_ref, wg_ref, wu_ref, wd_ref, o_ref, acc_ref, *, lane):
    f = pl.program_id(1)

    @pl.when(f == 0)
    def _():
        acc_ref[...] = jnp.zeros(acc_ref.shape, F32)

    x = x_ref[...]
    g = jnp.dot(x, wg_ref[0, 0].astype(BF16), preferred_element_type=F32)
    u = jnp.dot(x, wu_ref[0, 0].astype(BF16), preferred_element_type=F32)
    hdn = (g * jax.nn.sigmoid(g) * u).astype(BF16)
    acc_ref[...] += jnp.dot(hdn, wd_ref[0, 0].astype(BF16), preferred_element_type=F32)

    @pl.when(f == pl.num_programs(1) - 1)
    def _():
        o_ref[...] = (acc_ref[...] * gate_ref[:, lane:lane + 1]).astype(o_ref.dtype)


def expert_ffn(xe, gate, wg, wu, wd, layer, e, tm=1024, tf=256):
    cap, d = xe.shape
    ff = wg.shape[3]
    tm = _pick(cap, tm)
    tf = _pick(ff, tf)
    return pl.pallas_call(
        functools.partial(_expert_ffn_kernel, lane=e),
        grid=(cap // tm, ff // tf),
        in_specs=[
            pl.BlockSpec((tm, d), lambda i, f: (i, 0)),
            pl.BlockSpec((tm, LANES), lambda i, f: (i, 0)),
            pl.BlockSpec((1, 1, d, tf), lambda i, f: (layer, e, 0, f)),
            pl.BlockSpec((1, 1, d, tf), lambda i, f: (layer, e, 0, f)),
            pl.BlockSpec((1, 1, tf, d), lambda i, f: (layer, e, f, 0)),
        ],
        out_specs=pl.BlockSpec((tm, d), lambda i, f: (i, 0)),
        out_shape=jax.ShapeDtypeStruct((cap, d), BF16),
        scratch_shapes=[pltpu.VMEM((tm, d), F32)],
        compiler_params=_cparams(("parallel", "arbitrary")),
        name="expert_ffn",
    )(xe, gate, wg, wu, wd)


def _combine_kernel(blk_tbl, base_tbl, x_ref, key_ref, *rest, ne):
    o_refs, y_ref = rest[:ne], rest[ne]
    t, p = pl.program_id(0), pl.program_id(1)

    @pl.when(p == 0)
    def _():
        y_ref[...] = x_ref[...]

    bases = [base_tbl[_route_entry(t, p, e, ne)] for e in range(ne)]

    @pl.when(_any_active(bases, list(range(ne))))
    def _():
        key = key_ref[...]
        for g0 in range(0, ne, COMBINE_GROUP):
            grp = list(range(g0, g0 + COMBINE_GROUP))
            pmat = _one_hot(key, grp, bases)
            omat = jnp.concatenate([o_refs[e][...] for e in grp], axis=0)
            y_ref[...] += jnp.dot(pmat, omat, preferred_element_type=F32)


def combine(tables, x, key, outs):
    n, d = x.shape
    ne = len(outs)
    tok = lambda t, p, *tbls: (t, 0)
    row = lambda e: (lambda t, p, blk_tbl, *_: (blk_tbl[_route_entry(t, p, e, ne)], 0))
    return pl.pallas_call(
        functools.partial(_combine_kernel, ne=ne),
        grid_spec=pltpu.PrefetchScalarGridSpec(
            num_scalar_prefetch=2,
            grid=(n // TOKB, NPASS),
            in_specs=([pl.BlockSpec((TOKB, d), tok), pl.BlockSpec((TOKB, LANES), tok)]
                      + [pl.BlockSpec((ROWB, d), row(e)) for e in range(ne)]),
            out_specs=pl.BlockSpec((TOKB, d), tok),
        ),
        out_shape=jax.ShapeDtypeStruct((n, d), F32),
        compiler_params=_cparams(("arbitrary", "arbitrary")),
        name="moe_combine",
    )(*tables, x, key, *outs)


def ec_moe(x, g, router_w, wg, wu, wd, layer):
    n, d = x.shape
    ne = router_w.shape[1]
    cap = max(1, 2 * n // ne)
    assert n % TOKB == 0 and cap % ROWB == 0 and ne % ROUTE_GROUP == 0 and ne % COMBINE_GROUP == 0 and ne <= LANES
    xn, aff = router(x, g, router_w)
    thr, need = topk_threshold(aff, cap)
    key, starts = token_ranks(aff, thr, need)
    tables = _route_tables(starts[:, 0, :ne], cap)
    xes, gates = dispatch(tables, xn, key, aff, ne, cap)
    outs = [expert_ffn(xes[e], gates[e], wg, wu, wd, layer, e) for e in range(ne)]
    return combine(tables, x, key, outs)


def _trunk(x, p):
    b, s, d = x.shape
    n = b * s
    x = x.reshape(n, d)
    depth = p["norm_mix"].shape[0]
    for layer in range(depth):
        j = layer // 2
        if layer % 2 == 0:
            w_in = p["att_w_in"][j]
            a_qk = (w_in.shape[1] // 7)
            proj = norm_matmul(x, p["norm_mix"][layer], w_in, out_dtype=BF16).reshape(b, s, -1)
            lam_init = 0.8 - 0.6 * float(np.exp(-0.3 * layer))
            qk_a = qk_prep(proj, 0, a_qk, DA_QK_DIM, (p["att_q_norm"][j], p["att_k_norm"][j]),
                           (DA_QK_DIM ** -0.5 * math.log2(math.e), 1.0))
            lamv = jnp.stack([p["att_lam_q1"][j], p["att_lam_k1"][j], p["att_lam_q2"][j],
                              p["att_lam_k2"][j]]).astype(F32)
            out_a = diff_attention(qk_a, proj, 2 * a_qk, lamv, p["att_subln"][j].astype(F32), lam_init)
            qk_b = qk_prep(proj, 3 * a_qk, a_qk, HEAD_DIM, None, (1.0, HEAD_DIM ** -0.5))
            out_b = retention(qk_b, proj, 5 * a_qk, 6 * a_qk, p["ret_norm"][j])
            w_out = p["att_w_out"][j]
            x = matmul_residual([(out_a.reshape(n, -1), w_out[:a_qk]), (out_b.reshape(n, -1), w_out[a_qk:])], x)
        else:
            proj = norm_matmul(x, p["norm_mix"][layer], p["rg_w_in"][j], out_dtype=BF16).reshape(b, s, -1)
            y = rg_lru_block(proj, p["rg_conv_w"][j], p["rg_conv_b"][j], p["rg_wa"][j], p["rg_ba"][j],
                             p["rg_wi"][j], p["rg_bi"][j], p["rg_lambda"][j])
            x = matmul_residual([(y.reshape(n, -1), p["rg_w_out"][j])], x)
        x = ec_moe(x, p["norm_ffn"][layer], p["moe_router"][layer], p["moe_w_gate"], p["moe_w_up"],
                   p["moe_w_down"], layer)
    return x.reshape(b, s, d)


def kernel(x_prompt, x_sample, norm_mix, norm_ffn, att_w_in, att_w_out, att_q_norm, att_k_norm, att_lam_q1, att_lam_k1, att_lam_q2, att_lam_k2, att_subln, ret_norm, rg_w_in, rg_conv_w, rg_conv_b, rg_wa, rg_ba, rg_wi, rg_bi, rg_lambda, rg_w_out, moe_router, moe_w_gate, moe_w_up, moe_w_down):
    p = dict(
        norm_mix=norm_mix.astype(F32), norm_ffn=norm_ffn.astype(F32),
        att_w_in=att_w_in.astype(BF16), att_w_out=att_w_out.astype(BF16),
        att_q_norm=att_q_norm, att_k_norm=att_k_norm,
        att_lam_q1=att_lam_q1, att_lam_k1=att_lam_k1, att_lam_q2=att_lam_q2, att_lam_k2=att_lam_k2,
        att_subln=att_subln, ret_norm=ret_norm,
        rg_w_in=rg_w_in.astype(BF16), rg_conv_w=rg_conv_w, rg_conv_b=rg_conv_b,
        rg_wa=rg_wa, rg_ba=rg_ba, rg_wi=rg_wi, rg_bi=rg_bi, rg_lambda=rg_lambda,
        rg_w_out=rg_w_out.astype(BF16), moe_router=moe_router,
        moe_w_gate=moe_w_gate, moe_w_up=moe_w_up, moe_w_down=moe_w_down,
    )
    return (_trunk(x_prompt, p), _trunk(x_sample, p))
```

```python
import functools
import math

import numpy as np
import jax
import jax.numpy as jnp
from jax import lax
from jax.experimental import pallas as pl
from jax.experimental.pallas import tpu as pltpu

F32 = jnp.float32
BF16 = jnp.bfloat16

HEAD_DIM = 128
DA_QK_DIM = HEAD_DIM // 2
EPS = 1e-6
ROPE_THETA = 10000.0
RG_C = 8.0
CONV_WIDTH = 4
RET_CHUNK = 256
ATTN_ROW_BLOCK = 128
LANES = 128
SUBLANES = 8
HALO = 16
MXU_DIM = 256
VMEM_LIMIT = 52 * 1024 * 1024


def _cparams(sem):
    return pltpu.CompilerParams(dimension_semantics=sem, vmem_limit_bytes=VMEM_LIMIT)


def _pick(n, pref):
    t = min(pref, n)
    while n % t:
        t //= 2
    return t


def _norm_matmul_kernel(x_ref, g_ref, w_ref, o_ref, xn_ref):
    @pl.when(pl.program_id(1) == 0)
    def _():
        x = x_ref[...]
        ms = jnp.mean(x * x, axis=-1, keepdims=True)
        xn_ref[...] = (x * lax.rsqrt(ms + EPS) * g_ref[...]).astype(BF16)

    o_ref[...] = jnp.dot(xn_ref[...], w_ref[...], preferred_element_type=F32).astype(o_ref.dtype)


def norm_matmul(x, g, w, out_dtype=F32, tm=1024, tn=1024):
    n, d = x.shape
    nout = w.shape[1]
    tm = _pick(n, tm)
    tn = _pick(nout, tn)
    return pl.pallas_call(
        _norm_matmul_kernel,
        grid=(n // tm, nout // tn),
        in_specs=[
            pl.BlockSpec((tm, d), lambda i, j: (i, 0)),
            pl.BlockSpec((1, d), lambda i, j: (0, 0)),
            pl.BlockSpec((d, tn), lambda i, j: (0, j)),
        ],
        out_specs=pl.BlockSpec((tm, tn), lambda i, j: (i, j)),
        out_shape=jax.ShapeDtypeStruct((n, nout), out_dtype),
        scratch_shapes=[pltpu.VMEM((tm, d), BF16)],
        compiler_params=_cparams(("parallel", "arbitrary")),
        name="norm_matmul",
    )(x, g.reshape(1, d), w)


def _matmul_res_kernel(*refs, npairs):
    r_ref, o_ref = refs[2 * npairs], refs[2 * npairs + 1]
    acc = r_ref[...]
    for p in range(npairs):
        acc = acc + jnp.dot(refs[2 * p][...], refs[2 * p + 1][...], preferred_element_type=F32)
    o_ref[...] = acc


def matmul_residual(pairs, res, tm=1024, tn=1024):
    n, dout = res.shape
    tm = _pick(n, tm)
    tn = _pick(dout, tn)
    in_specs, args = [], []
    for y, w in pairs:
        k = y.shape[1]
        in_specs += [pl.BlockSpec((tm, k), lambda i, j: (i, 0)), pl.BlockSpec((k, tn), lambda i, j: (0, j))]
        args += [y, w]
    in_specs.append(pl.BlockSpec((tm, tn), lambda i, j: (i, j)))
    args.append(res)
    return pl.pallas_call(
        functools.partial(_matmul_res_kernel, npairs=len(pairs)),
        grid=(n // tm, dout // tn),
        in_specs=in_specs,
        out_specs=pl.BlockSpec((tm, tn), lambda i, j: (i, j)),
        out_shape=jax.ShapeDtypeStruct((n, dout), F32),
        compiler_params=_cparams(("parallel", "parallel")),
        name="matmul_residual",
    )(*args)


def _rope_tables(seq, group):
    half = group // 2
    inv = ROPE_THETA ** (-jnp.arange(half, dtype=F32) / half)
    ang = jnp.arange(seq, dtype=F32)[:, None] * inv[None, :]
    cos, sin = jnp.cos(ang), jnp.sin(ang)
    reps = LANES // group
    cos_t = jnp.tile(jnp.concatenate([cos, cos], axis=-1), (1, reps))
    sin_t = jnp.tile(jnp.concatenate([-sin, sin], axis=-1), (1, reps))
    return cos_t, sin_t


def _qk_prep_kernel(x_ref, cos_ref, sin_ref, g_ref, gmat_ref, o_ref, *, group, normed, scales):
    j = pl.program_id(2)
    scale = jnp.where(j == 0, scales[0], scales[1]).astype(F32)
    cos = cos_ref[...]
    sin = sin_ref[...]
    g = g_ref[0]
    nh = x_ref.shape[2] // LANES
    lane = lax.broadcasted_iota(jnp.int32, cos.shape, 1)
    first_half = (lane % group) < (group // 2)
    for h in range(nh):
        x = x_ref[0, :, h * LANES:(h + 1) * LANES].astype(F32)
        if normed:
            sq = x * x
            hi = sq.astype(BF16)
            lo = (sq - hi.astype(F32)).astype(BF16)
            ms = (jnp.dot(hi, gmat_ref[...], preferred_element_type=F32)
                  + jnp.dot(lo, gmat_ref[...], preferred_element_type=F32))
            x = x * lax.rsqrt(ms + EPS) * g
        if group == LANES:
            swapped = pltpu.roll(x, LANES // 2, axis=1)
        else:
            fwd = pltpu.roll(x, group // 2, axis=1)
            bwd = pltpu.roll(x, LANES - group // 2, axis=1)
            swapped = jnp.where(first_half, bwd, fwd)
        y = (x * cos + swapped * sin) * scale
        o_ref[0, :, h * LANES:(h + 1) * LANES] = y.astype(o_ref.dtype)


def qk_prep(proj, col0, width, group, norm_w, scales, ts=512):
    b, s, _ = proj.shape
    ts = _pick(s, ts)
    cos_t, sin_t = _rope_tables(s, group)
    normed = norm_w is not None
    if normed:
        g = jnp.stack([jnp.tile(w.astype(F32), LANES // group) for w in norm_w]).reshape(2, 1, LANES)
    else:
        g = jnp.ones((2, 1, LANES), F32)
    grp = np.arange(LANES) // group
    gmat = jnp.asarray((grp[:, None] == grp[None, :]).astype(np.float32) / group, BF16)
    cb = col0 // width
    return pl.pallas_call(
        functools.partial(_qk_prep_kernel, group=group, normed=normed, scales=scales),
        grid=(b, s // ts, 2),
        in_specs=[
            pl.BlockSpec((1, ts, width), lambda bi, si, j: (bi, si, cb + j)),
            pl.BlockSpec((ts, LANES), lambda bi, si, j: (si, 0)),
            pl.BlockSpec((ts, LANES), lambda bi, si, j: (si, 0)),
            pl.BlockSpec((1, 1, LANES), lambda bi, si, j: (j, 0, 0)),
            pl.BlockSpec((LANES, LANES), lambda bi, si, j: (0, 0)),
        ],
        out_specs=pl.BlockSpec((1, ts, width), lambda bi, si, j: (bi, si, j)),
        out_shape=jax.ShapeDtypeStruct((b, s, 2 * width), BF16),
        compiler_params=_cparams(("parallel", "parallel", "parallel")),
        name="qk_prep",
    )(proj, cos_t, sin_t, g, gmat)


def _diff_attn_kernel(q_ref, k_ref, v_ref, lamv_ref, sub_ref, o_ref, m1, a1, m2, a2, *, lam_init):
    ki = pl.program_id(3)

    @pl.when(ki == 0)
    def _():
        for m, a in ((m1, a1), (m2, a2)):
            m[...] = jnp.full(m.shape, -jnp.inf, F32)
            a[...] = jnp.zeros(a.shape, F32)

    q = q_ref[0]
    k = k_ref[0]
    v = v_ref[0].astype(BF16)
    vlane = lax.broadcasted_iota(jnp.int32, v.shape, 1)
    v_ext = jnp.concatenate([v, jnp.where(vlane == 0, 1.0, 0.0).astype(BF16)], axis=1)
    lane = lax.broadcasted_iota(jnp.int32, q.shape, 1)
    zero = jnp.zeros_like(q)
    q_parts = (jnp.where(lane < DA_QK_DIM, q, zero), jnp.where(lane >= DA_QK_DIM, q, zero))
    rb = min(ATTN_ROW_BLOCK, q.shape[0])
    for r0 in range(0, q.shape[0], rb):
        rows = slice(r0, r0 + rb)
        for qq, m, a in ((q_parts[0], m1, a1), (q_parts[1], m2, a2)):
            s = lax.dot_general(qq[rows], k, (((1,), (1,)), ((), ())), preferred_element_type=F32)
            m_old = m[rows]
            m_new = jnp.maximum(m_old, jnp.max(s, axis=-1, keepdims=True))
            p = jnp.exp2(s - m_new).astype(BF16)
            a[rows] = jnp.exp2(m_old - m_new) * a[rows] + jnp.dot(p, v_ext, preferred_element_type=F32)
            m[rows] = m_new

    @pl.when(ki == pl.num_programs(3) - 1)
    def _():
        lv = lamv_ref[...]
        lam = (jnp.exp(jnp.sum(lv[0:1] * lv[1:2], axis=-1, keepdims=True))
               - jnp.exp(jnp.sum(lv[2:3] * lv[3:4], axis=-1, keepdims=True)) + lam_init)
        acc1, acc2 = a1[...], a2[...]
        o = (acc1[:, :HEAD_DIM] / acc1[:, HEAD_DIM:HEAD_DIM + 1]
             - lam * (acc2[:, :HEAD_DIM] / acc2[:, HEAD_DIM:HEAD_DIM + 1]))
        ms = jnp.mean(o * o, axis=-1, keepdims=True)
        y = o * lax.rsqrt(ms + EPS) * sub_ref[0]
        o_ref[0] = (y * (1.0 - lam_init)).astype(o_ref.dtype)


def diff_attention(qk, proj, v_col0, lamv, subln, lam_init, tq=1024, tk=4096):
    b, s, two_a = qk.shape
    nh = two_a // 2 // HEAD_DIM
    tq = _pick(s, tq)
    tk = _pick(s, tk)
    vb = v_col0 // HEAD_DIM
    return pl.pallas_call(
        functools.partial(_diff_attn_kernel, lam_init=lam_init),
        grid=(b, nh, s // tq, s // tk),
        in_specs=[
            pl.BlockSpec((1, tq, HEAD_DIM), lambda bi, h, qi, ki: (bi, qi, h)),
            pl.BlockSpec((1, tk, HEAD_DIM), lambda bi, h, qi, ki: (bi, ki, nh + h)),
            pl.BlockSpec((1, tk, HEAD_DIM), lambda bi, h, qi, ki: (bi, ki, vb + h)),
            pl.BlockSpec((4, DA_QK_DIM), lambda bi, h, qi, ki: (0, 0)),
            pl.BlockSpec((1, 1, HEAD_DIM), lambda bi, h, qi, ki: (h, 0, 0)),
        ],
        out_specs=pl.BlockSpec((1, tq, HEAD_DIM), lambda bi, h, qi, ki: (bi, qi, h)),
        out_shape=jax.ShapeDtypeStruct((b, s, nh * HEAD_DIM), BF16),
        scratch_shapes=[pltpu.VMEM((tq, 1), F32), pltpu.VMEM((tq, 2 * HEAD_DIM), F32),
                        pltpu.VMEM((tq, 1), F32), pltpu.VMEM((tq, 2 * HEAD_DIM), F32)],
        compiler_params=_cparams(("parallel", "parallel", "parallel", "arbitrary")),
        name="diff_attention",
    )(qk, qk, proj, lamv, subln.reshape(nh, 1, HEAD_DIM))


def _retention_consts(nh, reverse):
    c = RET_CHUNK
    log_g = np.log1p(-(2.0 ** (-5.0 - np.arange(nh, dtype=np.float64))))
    pos = np.arange(c, dtype=np.float64)
    diff = pos[:, None] - pos[None, :]
    if reverse:
        mask = diff < 0
        dmat = np.where(mask[None], np.exp(np.where(mask, -diff, 0.0)[None] * log_g[:, None, None]), 0.0)
        qdec = np.exp((c - pos)[None, :] * log_g[:, None])
        kdec = np.exp(pos[None, :] * log_g[:, None])
    else:
        mask = diff >= 0
        dmat = np.where(mask[None], np.exp(np.where(mask, diff, 0.0)[None] * log_g[:, None, None]), 0.0)
        qdec = np.exp((pos + 1.0)[None, :] * log_g[:, None])
        kdec = np.exp((c - 1 - pos)[None, :] * log_g[:, None])
    cdec = np.exp(c * log_g)
    bc = lambda a: jnp.asarray(np.broadcast_to(a[:, :, None], (nh, c, HEAD_DIM)).astype(np.float32))
    return jnp.asarray(dmat.astype(np.float32)), bc(qdec), bc(kdec), [float(x) for x in cdec]


def _retention_kernel(*refs, reverse, cdec, nh):
    if reverse:
        q_ref, k_ref, v_ref, d_ref, qd_ref, kd_ref, f_ref, gb_ref, nw_ref, o_ref, r_ref = refs
    else:
        q_ref, k_ref, v_ref, d_ref, qd_ref, kd_ref, o_ref, r_ref = refs

    @pl.when(pl.program_id(1) == 0)
    def _():
        r_ref[...] = jnp.zeros(r_ref.shape, F32)

    c = RET_CHUNK
    nchunk = q_ref.shape[1] // c
    order = range(nchunk - 1, -1, -1) if reverse else range(nchunk)
    for h in range(nh):
        cols = slice(h * HEAD_DIM, (h + 1) * HEAD_DIM)
        dmat = d_ref[h]
        qd = qd_ref[h]
        kd = kd_ref[h]
        for ci in order:
            rows = slice(ci * c, (ci + 1) * c)
            q = q_ref[0, rows, cols]
            k = k_ref[0, rows, cols]
            v = v_ref[0, rows, cols].astype(F32)
            s = lax.dot_general(q, k, (((1,), (1,)), ((), ())), preferred_element_type=F32) * dmat
            inner = jnp.dot(s.astype(BF16), v.astype(BF16), preferred_element_type=F32)
            r_prev = r_ref[h]
            cross = jnp.dot(q, r_prev.astype(BF16), preferred_element_type=F32) * qd
            kv = lax.dot_general(k, (v * kd).astype(BF16), (((0,), (0,)), ((), ())),
                                 preferred_element_type=F32)
            r_ref[h] = r_prev * cdec[h] + kv
            out = inner + cross
            if reverse:
                ret = out + f_ref[0, rows, cols]
                ms = jnp.mean(ret * ret, axis=-1, keepdims=True)
                y = ret * lax.rsqrt(ms + EPS) * nw_ref[h]
                gate = gb_ref[0, rows, cols].astype(F32)
                y = y * (gate * jax.nn.sigmoid(gate))
                o_ref[0, rows, cols] = y.astype(o_ref.dtype)
            else:
                o_ref[0, rows, cols] = out


def retention(qk, proj, v_col0, g_col0, ret_norm, tr=512):
    b, s, two_w = qk.shape
    w = two_w // 2
    nh = w // HEAD_DIM
    tr = _pick(s, tr)
    nt = s // tr
    vb, gbk = v_col0 // w, g_col0 // w
    outs = None
    for reverse in (False, True):
        dmat, qd, kd, cdec = _retention_consts(nh, reverse)
        tmap = (lambda t: nt - 1 - t) if reverse else (lambda t: t)
        row = lambda cb: (lambda bi, t: (bi, tmap(t), cb))
        const3 = lambda bi, t: (0, 0, 0)
        in_specs = [
            pl.BlockSpec((1, tr, w), row(0)),
            pl.BlockSpec((1, tr, w), row(1)),
            pl.BlockSpec((1, tr, w), row(vb)),
            pl.BlockSpec((nh, RET_CHUNK, RET_CHUNK), const3),
            pl.BlockSpec((nh, RET_CHUNK, HEAD_DIM), const3),
            pl.BlockSpec((nh, RET_CHUNK, HEAD_DIM), const3),
        ]
        args = [qk, qk, proj, dmat, qd, kd]
        if reverse:
            in_specs += [pl.BlockSpec((1, tr, w), row(0)), pl.BlockSpec((1, tr, w), row(gbk)),
                         pl.BlockSpec((nh, 1, HEAD_DIM), const3)]
            args += [outs, proj, ret_norm.astype(F32).reshape(nh, 1, HEAD_DIM)]
        outs = pl.pallas_call(
            functools.partial(_retention_kernel, reverse=reverse, cdec=cdec, nh=nh),
            grid=(b, nt),
            in_specs=in_specs,
            out_specs=pl.BlockSpec((1, tr, w), row(0)),
            out_shape=jax.ShapeDtypeStruct((b, s, w), BF16 if reverse else F32),
            scratch_shapes=[pltpu.VMEM((nh, HEAD_DIM, HEAD_DIM), F32)],
            compiler_params=_cparams(("parallel", "arbitrary")),
            name="retention_bwd" if reverse else "retention_fwd",
        )(*args)
    return outs


def _rg_slab_plan(d_rnn, bs):
    tiles = []
    c0 = 0
    while c0 < d_rnn:
        c1 = min(c0 + MXU_DIM, d_rnn)
        r0 = (c0 // bs) * bs
        r1 = ((c1 - 1) // bs + 1) * bs
        k0 = (r0 // LANES) * LANES
        k1 = min(-(-r1 // LANES) * LANES, d_rnn)
        tiles.append((c0, c1, k0, k1))
        c0 = c1
    kmax = max(k1 - k0 for _, _, k0, k1 in tiles)
    plan = [(c0, c1, min(k0, d_rnn - kmax)) for c0, c1, k0, _ in tiles]
    return plan, kmax


def _rg_slabs(w_blocks, plan, kmax):
    dense = jax.scipy.linalg.block_diag(*[w_blocks[i] for i in range(w_blocks.shape[0])])
    slabs = []
    for c0, c1, k0 in plan:
        sl = dense[k0:k0 + kmax, c0:c1]
        if c1 - c0 < MXU_DIM:
            sl = jnp.pad(sl, ((0, 0), (0, MXU_DIM - (c1 - c0))))
        slabs.append(sl)
    return jnp.stack(slabs).astype(BF16)


def _sigmoid(z):
    return 0.5 * jnp.tanh(0.5 * z) + 0.5


def _rg_kernel(*refs, reverse, plan, kmax):
    if reverse:
        (x_ref, xp_ref, xn_ref, cw_ref, cb_ref, wa_ref, wi_ref, ba_ref, bi_ref, lam_ref,
         g_ref, hf_ref, o_ref, a_s, b_s, carry) = refs
    else:
        (x_ref, xp_ref, xn_ref, cw_ref, cb_ref, wa_ref, wi_ref, ba_ref, bi_ref, lam_ref,
         o_ref, a_s, b_s, carry) = refs
    t = pl.program_id(1)
    nt = pl.num_programs(1)
    tchunk = (nt - 1 - t) if reverse else t

    @pl.when(t == 0)
    def _():
        carry[...] = jnp.zeros(carry.shape, F32)

    ts, c = x_ref.shape[1], x_ref.shape[2]
    xm = x_ref[0].astype(F32)
    xp = jnp.where(tchunk == 0, 0.0, xp_ref[0].astype(F32))
    xn = jnp.where(tchunk == nt - 1, 0.0, xn_ref[0].astype(F32))
    xe = jnp.concatenate([xp, xm, xn], axis=0)
    ext = ts + 2 * HALO
    win = slice(HALO, HALO + ts)
    cw = cw_ref[...]
    xc = (cw[0:1] * pltpu.roll(xe, 2, axis=0)[win] + cw[1:2] * pltpu.roll(xe, 1, axis=0)[win]
          + cw[2:3] * xm + cw[3:4] * pltpu.roll(xe, ext - 1, axis=0)[win] + cb_ref[...])
    xb = xc.astype(BF16)
    lam = lam_ref[...]
    sp = jnp.maximum(-lam, 0.0) + jnp.log1p(jnp.exp(-jnp.abs(lam)))
    for ti, (c0, c1, k0) in enumerate(plan):
        xs = xb[:, k0:k0 + kmax]
        wdt = c1 - c0
        r = _sigmoid(jnp.dot(xs, wa_ref[ti], preferred_element_type=F32)[:, :wdt] + ba_ref[:, c0:c1])
        i = _sigmoid(jnp.dot(xs, wi_ref[ti], preferred_element_type=F32)[:, :wdt] + bi_ref[:, c0:c1])
        log_a = (-RG_C) * sp[:, c0:c1] * r
        a = jnp.exp(log_a)
        a_s[:, c0:c1] = a
        b_s[:, c0:c1] = jnp.sqrt(1.0 - a * a) * (i * xc[:, c0:c1])

    ngroups = ts // SUBLANES
    rowid = lax.broadcasted_iota(jnp.int32, (SUBLANES, c), 0)

    def group(gi, h):
        g = (ngroups - 1 - gi) if reverse else gi
        rows = pl.ds(pl.multiple_of(g * SUBLANES, SUBLANES), SUBLANES)
        a = a_s[rows, :]
        b = b_s[rows, :]
        for k in (1, 2, 4):
            if reverse:
                valid = rowid < SUBLANES - k
                sh = SUBLANES - k
            else:
                valid = rowid >= k
                sh = k
            b = b + a * jnp.where(valid, pltpu.roll(b, sh, axis=0), 0.0)
            a = a * jnp.where(valid, pltpu.roll(a, sh, axis=0), 1.0)
        hg = b + a * h
        b_s[rows, :] = hg
        edge = hg[0:1] if reverse else hg[SUBLANES - 1:SUBLANES]
        return jnp.broadcast_to(edge, (SUBLANES, c))

    carry[...] = lax.fori_loop(0, ngroups, group, carry[...])
    if reverse:
        gate = g_ref[0].astype(F32)
        o_ref[0] = (jax.nn.gelu(gate) * (hf_ref[0] + b_s[...])).astype(o_ref.dtype)
    else:
        o_ref[0] = b_s[...]


def rg_lru_block(proj, conv_w, conv_b, wa, ba, wi, bi, lam, ts=256):
    b, s, two_c = proj.shape
    c = two_c // 2
    nb, bs = wa.shape[1], wa.shape[2]
    ts = _pick(s, ts)
    nt = s // ts
    hb = ts // HALO
    plan, kmax = _rg_slab_plan(c, bs)
    out = None
    for reverse in (False, True):
        d = 1 if reverse else 0
        tmap = (lambda t: nt - 1 - t) if reverse else (lambda t: t)
        const2 = lambda bi_, t: (0, 0)
        const3 = lambda bi_, t: (0, 0, 0)
        in_specs = [
            pl.BlockSpec((1, ts, c), lambda bi_, t: (bi_, tmap(t), 1)),
            pl.BlockSpec((1, HALO, c), lambda bi_, t: (bi_, jnp.maximum(tmap(t) * hb - 1, 0), 1)),
            pl.BlockSpec((1, HALO, c), lambda bi_, t: (bi_, jnp.minimum((tmap(t) + 1) * hb, nt * hb - 1), 1)),
            pl.BlockSpec((CONV_WIDTH, c), const2),
            pl.BlockSpec((1, c), const2),
            pl.BlockSpec((len(plan), kmax, MXU_DIM), const3),
            pl.BlockSpec((len(plan), kmax, MXU_DIM), const3),
            pl.BlockSpec((1, c), const2),
            pl.BlockSpec((1, c), const2),
            pl.BlockSpec((1, c), const2),
        ]
        args = [proj, proj, proj, conv_w.astype(F32), conv_b.astype(F32).reshape(1, c),
                _rg_slabs(wa[d], plan, kmax), _rg_slabs(wi[d], plan, kmax),
                ba[d].astype(F32).reshape(1, c), bi[d].astype(F32).reshape(1, c), lam[d].astype(F32).reshape(1, c)]
        if reverse:
            in_specs += [pl.BlockSpec((1, ts, c), lambda bi_, t: (bi_, tmap(t), 0)),
                         pl.BlockSpec((1, ts, c), lambda bi_, t: (bi_, tmap(t), 0))]
            args += [proj, out]
        out = pl.pallas_call(
            functools.partial(_rg_kernel, reverse=reverse, plan=plan, kmax=kmax),
            grid=(b, nt),
            in_specs=in_specs,
            out_specs=pl.BlockSpec((1, ts, c), lambda bi_, t: (bi_, tmap(t), 0)),
            out_shape=jax.ShapeDtypeStruct((b, s, c), BF16 if reverse else F32),
            scratch_shapes=[pltpu.VMEM((ts, c), F32), pltpu.VMEM((ts, c), F32), pltpu.VMEM((SUBLANES, c), F32)],
            compiler_params=_cparams(("parallel", "arbitrary")),
            name="rg_lru_bwd" if reverse else "rg_lru_fwd",
        )(*args)
    return out


TOKB = 512
ROWB = 128
NPASS = (ROWB - 1 + TOKB - 1) // ROWB + 1
ROUTE_GROUP = 4
COMBINE_GROUP = 8


def _router_kernel(x_ref, g_ref, r_ref, xn_ref, aff_ref, *, ne):
    x = x_ref[...]
    ms = jnp.mean(x * x, axis=-1, keepdims=True)
    xn = x * lax.rsqrt(ms + EPS) * g_ref[...]
    xn_ref[...] = xn.astype(BF16)
    logits = jnp.dot(xn, r_ref[...], precision=lax.Precision.HIGHEST, preferred_element_type=F32)
    lane = lax.broadcasted_iota(jnp.int32, logits.shape, 1)
    logits = jnp.where(lane < ne, logits, -jnp.inf)
    e = jnp.exp(logits - jnp.max(logits, axis=-1, keepdims=True))
    aff_ref[...] = e / jnp.sum(e, axis=-1, keepdims=True)


def router(x, g, router_w, tm=512):
    n, d = x.shape
    ne = router_w.shape[1]
    tm = _pick(n, tm)
    rw = jnp.pad(router_w.astype(F32), ((0, 0), (0, LANES - ne)))
    return pl.pallas_call(
        functools.partial(_router_kernel, ne=ne),
        grid=(n // tm,),
        in_specs=[pl.BlockSpec((tm, d), lambda i: (i, 0)), pl.BlockSpec((1, d), lambda i: (0, 0)),
                  pl.BlockSpec((d, LANES), lambda i: (0, 0))],
        out_specs=[pl.BlockSpec((tm, d), lambda i: (i, 0)), pl.BlockSpec((tm, LANES), lambda i: (i, 0))],
        out_shape=[jax.ShapeDtypeStruct((n, d), BF16), jax.ShapeDtypeStruct((n, LANES), F32)],
        compiler_params=_cparams(("parallel",)),
        name="router",
    )(x, g.reshape(1, d), rw)


def _topk_threshold_kernel(a_ref, thr_ref, need_ref, *, cap, chunk):
    nch = a_ref.shape[0] // chunk

    def count(pred):
        def body(i, acc):
            x = a_ref[pl.ds(pl.multiple_of(i * chunk, chunk), chunk), :]
            return acc + jnp.sum(jnp.where(pred(x), 1.0, 0.0), axis=0, keepdims=True)
        return lax.fori_loop(0, nch, body, jnp.zeros((1, LANES), F32))

    def bit_step(k, v):
        cand = v | lax.shift_left(jnp.int32(1), 30 - k)
        candf = lax.bitcast_convert_type(cand, F32)
        cnt = count(lambda x: x >= candf)
        return jnp.where(cnt >= cap, cand, v)

    v = lax.fori_loop(0, 31, bit_step, jnp.zeros((1, LANES), jnp.int32))
    thr = lax.bitcast_convert_type(v, F32)
    thr_ref[...] = thr
    need_ref[...] = cap - count(lambda x: x > thr)


def topk_threshold(aff, cap):
    n = aff.shape[0]
    vspec = pl.BlockSpec(memory_space=pltpu.VMEM)
    return pl.pallas_call(
        functools.partial(_topk_threshold_kernel, cap=float(cap), chunk=_pick(n, 512)),
        in_specs=[vspec],
        out_specs=[vspec, vspec],
        out_shape=[jax.ShapeDtypeStruct((1, LANES), F32), jax.ShapeDtypeStruct((1, LANES), F32)],
        compiler_params=pltpu.CompilerParams(vmem_limit_bytes=VMEM_LIMIT),
        name="topk_threshold",
    )(aff)


def _rank_kernel(a_ref, thr_ref, need_ref, tri_ref, key_ref, start_ref, ceq, csel):
    @pl.when(pl.program_id(0) == 0)
    def _():
        ceq[...] = jnp.zeros(ceq.shape, F32)
        csel[...] = jnp.zeros(csel.shape, F32)

    x = a_ref[...]
    thr = thr_ref[...]
    gt = jnp.where(x > thr, 1.0, 0.0)
    eq = jnp.where(x == thr, 1.0, 0.0)
    tri = tri_ref[...]
    eq_incl = jnp.dot(tri, eq.astype(BF16), preferred_element_type=F32)
    eq_rank = ceq[...] + eq_incl - eq
    sel = jnp.maximum(gt, eq * jnp.where(eq_rank < need_ref[...], 1.0, 0.0))
    sel_incl = jnp.dot(tri, sel.astype(BF16), preferred_element_type=F32)
    rank = csel[...] + sel_incl - 1.0
    key_ref[...] = jnp.where(sel > 0.0, rank, -1.0).astype(jnp.int32)
    start_ref[0] = jnp.broadcast_to(csel[...], (SUBLANES, LANES)).astype(jnp.int32)
    nrow = x.shape[0]
    ceq[...] += eq_incl[nrow - 1:nrow]
    csel[...] += sel_incl[nrow - 1:nrow]


def token_ranks(aff, thr, need):
    n = aff.shape[0]
    nt = n // TOKB
    tri = jnp.asarray(np.tril(np.ones((TOKB, TOKB), np.float32)), BF16)
    return pl.pallas_call(
        _rank_kernel,
        grid=(nt,),
        in_specs=[pl.BlockSpec((TOKB, LANES), lambda t: (t, 0)), pl.BlockSpec((1, LANES), lambda t: (0, 0)),
                  pl.BlockSpec((1, LANES), lambda t: (0, 0)), pl.BlockSpec((TOKB, TOKB), lambda t: (0, 0))],
        out_specs=[pl.BlockSpec((TOKB, LANES), lambda t: (t, 0)),
                   pl.BlockSpec((1, SUBLANES, LANES), lambda t: (t, 0, 0))],
        out_shape=[jax.ShapeDtypeStruct((n, LANES), jnp.int32),
                   jax.ShapeDtypeStruct((nt, SUBLANES, LANES), jnp.int32)],
        scratch_shapes=[pltpu.VMEM((1, LANES), F32), pltpu.VMEM((1, LANES), F32)],
        compiler_params=_cparams(("arbitrary",)),
        name="token_ranks",
    )(aff, thr, need, tri)


def _route_tables(starts, cap):
    nblk = cap // ROWB
    s0 = starts[:, None, :]
    s1 = jnp.concatenate([starts[1:], jnp.full_like(starts[:1], cap)], axis=0)[:, None, :]
    b0 = s0 // ROWB
    b1 = jnp.maximum(s1 - 1, s0) // ROWB
    p = jnp.arange(NPASS, dtype=jnp.int32)[None, :, None]
    blk = jnp.minimum(jnp.minimum(b0 + p, b1), nblk - 1)
    active = (b0 + p) <= b1
    base = jnp.where(active, blk * ROWB, -(2 ** 30))
    return blk.reshape(-1).astype(jnp.int32), base.reshape(-1).astype(jnp.int32)


def _route_entry(t, p, e, ne):
    return (t * NPASS + p) * ne + e


def _one_hot(key, experts, bases):
    per = LANES // ROWB
    lane = lax.broadcasted_iota(jnp.int32, (key.shape[0], LANES), 1)
    col = lane % ROWB
    groups = []
    for g0 in range(0, len(experts), per):
        ke = None
        for j, e in enumerate(experts[g0:g0 + per]):
            kj = jnp.broadcast_to(key[:, e:e + 1], lane.shape) - bases[e]
            ke = kj if ke is None else jnp.where(lane >= j * ROWB, kj, ke)
        groups.append(jnp.where(ke == col, 1.0, 0.0).astype(BF16))
    return groups[0] if len(groups) == 1 else jnp.concatenate(groups, axis=1)


def _any_active(bases, experts):
    flag = bases[experts[0]] >= 0
    for e in experts[1:]:
        flag = flag | (bases[e] >= 0)
    return flag


def _dispatch_kernel(blk_tbl, base_tbl, xn_ref, key_ref, aff_ref, *rest, ne):
    xe_refs, gate_refs, last = rest[:ne], rest[ne:2 * ne], rest[2 * ne]
    t, p = pl.program_id(0), pl.program_id(1)

    @pl.when((t == 0) & (p == 0))
    def _():
        for e in range(ne):
            last[e] = -1

    bases = []
    for e in range(ne):
        blk = blk_tbl[_route_entry(t, p, e, ne)]

        @pl.when(blk != last[e])
        def _():
            xe_refs[e][...] = jnp.zeros(xe_refs[e].shape, BF16)
            gate_refs[e][...] = jnp.zeros(gate_refs[e].shape, F32)

        last[e] = blk
        bases.append(base_tbl[_route_entry(t, p, e, ne)])

    @pl.when(_any_active(bases, list(range(ne))))
    def _():
        key = key_ref[...]
        xn = xn_ref[...]
        a = aff_ref[...]
        a1 = a.astype(BF16)
        r1 = a - a1.astype(F32)
        a2 = r1.astype(BF16)
        a3 = (r1 - a2.astype(F32)).astype(BF16)
        tn = (((0,), (0,)), ((), ()))
        for g0 in range(0, ne, ROUTE_GROUP):
            grp = list(range(g0, g0 + ROUTE_GROUP))
            pmat = _one_hot(key, grp, bases)
            res = lax.dot_general(pmat, xn, tn, preferred_element_type=F32)
            gres = (lax.dot_general(pmat, a1, tn, preferred_element_type=F32)
                    + lax.dot_general(pmat, a2, tn, preferred_element_type=F32)
                    + lax.dot_general(pmat, a3, tn, preferred_element_type=F32))
            for j, e in enumerate(grp):
                rows = slice(j * ROWB, (j + 1) * ROWB)
                xe_refs[e][...] += res[rows].astype(BF16)
                gate_refs[e][...] += gres[rows]


def dispatch(tables, xn, key, aff, ne, cap):
    n, d = xn.shape
    kern = functools.partial(_dispatch_kernel, ne=ne)
    tok = lambda t, p, *tbls: (t, 0)
    row = lambda e: (lambda t, p, blk_tbl, *_: (blk_tbl[_route_entry(t, p, e, ne)], 0))
    outs = pl.pallas_call(
        kern,
        grid_spec=pltpu.PrefetchScalarGridSpec(
            num_scalar_prefetch=2,
            grid=(n // TOKB, NPASS),
            in_specs=[pl.BlockSpec((TOKB, d), tok), pl.BlockSpec((TOKB, LANES), tok),
                      pl.BlockSpec((TOKB, LANES), tok)],
            out_specs=([pl.BlockSpec((ROWB, d), row(e)) for e in range(ne)]
                       + [pl.BlockSpec((ROWB, LANES), row(e)) for e in range(ne)]),
            scratch_shapes=[pltpu.SMEM((ne,), jnp.int32)],
        ),
        out_shape=([jax.ShapeDtypeStruct((cap, d), BF16)] * ne + [jax.ShapeDtypeStruct((cap, LANES), F32)] * ne),
        compiler_params=_cparams(("arbitrary", "arbitrary")),
        name="moe_dispatch",
    )(*tables, xn, key, aff)
    return outs[:ne], outs[ne:]


def _expert_ffn_kernel(x_ref, gate_ref, wg_ref, wu_ref, wd_ref, o_ref, acc_ref, *, lane):
    f = pl.program_id(1)

    @pl.when(f == 0)
    def _():
        acc_ref[...] = jnp.zeros(acc_ref.shape, F32)

    x = x_ref[...]
    g = jnp.dot(x, wg_ref[0, 0].astype(BF16), preferred_element_type=F32)
    u = jnp.dot(x, wu_ref[0, 0].astype(BF16), preferred_element_type=F32)
    hdn = (g * jax.nn.sigmoid(g) * u).astype(BF16)
    acc_ref[...] += jnp.dot(hdn, wd_ref[0, 0].astype(BF16), preferred_element_type=F32)

    @pl.when(f == pl.num_programs(1) - 1)
    def _():
        o_ref[...] = (acc_ref[...] * gate_ref[:, lane:lane + 1]).astype(o_ref.dtype)


def expert_ffn(xe, gate, wg, wu, wd, layer, e, tm=1024, tf=256):
    cap, d = xe.shape
    ff = wg.shape[3]
    tm = _pick(cap, tm)
    tf = _pick(ff, tf)
    return pl.pallas_call(
        functools.partial(_expert_ffn_kernel, lane=e),
        grid=(cap // tm, ff // tf),
        in_specs=[
            pl.BlockSpec((tm, d), lambda i, f: (i, 0)),
            pl.BlockSpec((tm, LANES), lambda i, f: (i, 0)),
            pl.BlockSpec((1, 1, d, tf), lambda i, f: (layer, e, 0, f)),
            pl.BlockSpec((1, 1, d, tf), lambda i, f: (layer, e, 0, f)),
            pl.BlockSpec((1, 1, tf, d), lambda i, f: (layer, e, f, 0)),
        ],
        out_specs=pl.BlockSpec((tm, d), lambda i, f: (i, 0)),
        out_shape=jax.ShapeDtypeStruct((cap, d), BF16),
        scratch_shapes=[pltpu.VMEM((tm, d), F32)],
        compiler_params=_cparams(("parallel", "arbitrary")),
        name="expert_ffn",
    )(xe, gate, wg, wu, wd)


def _combine_kernel(blk_tbl, base_tbl, x_ref, key_ref, *rest, ne):
    o_refs, y_ref = rest[:ne], rest[ne]
    t, p = pl.program_id(0), pl.program_id(1)

    @pl.when(p == 0)
    def _():
        y_ref[...] = x_ref[...]

    bases = [base_tbl[_route_entry(t, p, e, ne)] for e in range(ne)]

    @pl.when(_any_active(bases, list(range(ne))))
    def _():
        key = key_ref[...]
        for g0 in range(0, ne, COMBINE_GROUP):
            grp = list(range(g0, g0 + COMBINE_GROUP))
            pmat = _one_hot(key, grp, bases)
            omat = jnp.concatenate([o_refs[e][...] for e in grp], axis=0)
            y_ref[...] += jnp.dot(pmat, omat, preferred_element_type=F32)


def combine(tables, x, key, outs):
    n, d = x.shape
    ne = len(outs)
    tok = lambda t, p, *tbls: (t, 0)
    row = lambda e: (lambda t, p, blk_tbl, *_: (blk_tbl[_route_entry(t, p, e, ne)], 0))
    return pl.pallas_call(
        functools.partial(_combine_kernel, ne=ne),
        grid_spec=pltpu.PrefetchScalarGridSpec(
            num_scalar_prefetch=2,
            grid=(n // TOKB, NPASS),
            in_specs=([pl.BlockSpec((TOKB, d), tok), pl.BlockSpec((TOKB, LANES), tok)]
                      + [pl.BlockSpec((ROWB, d), row(e)) for e in range(ne)]),
            out_specs=pl.BlockSpec((TOKB, d), tok),
        ),
        out_shape=jax.ShapeDtypeStruct((n, d), F32),
        compiler_params=_cparams(("arbitrary", "arbitrary")),
        name="moe_combine",
    )(*tables, x, key, *outs)


def ec_moe(x, g, router_w, wg, wu, wd, layer):
    n, d = x.shape
    ne = router_w.shape[1]
    cap = max(1, 2 * n // ne)
    assert n % TOKB == 0 and cap % ROWB == 0 and ne % ROUTE_GROUP == 0 and ne % COMBINE_GROUP == 0 and ne <= LANES
    xn, aff = router(x, g, router_w)
    thr, need = topk_threshold(aff, cap)
    key, starts = token_ranks(aff, thr, need)
    tables = _route_tables(starts[:, 0, :ne], cap)
    xes, gates = dispatch(tables, xn, key, aff, ne, cap)
    outs = [expert_ffn(xes[e], gates[e], wg, wu, wd, layer, e) for e in range(ne)]
    return combine(tables, x, key, outs)


def _trunk(x, p):
    b, s, d = x.shape
    n = b * s
    x = x.reshape(n, d)
    depth = p["norm_mix"].shape[0]
    for layer in range(depth):
        j = layer // 2
        if layer % 2 == 0:
            w_in = p["att_w_in"][j]
            a_qk = (w_in.shape[1] // 7)
            proj = norm_matmul(x, p["norm_mix"][layer], w_in, out_dtype=BF16).reshape(b, s, -1)
            lam_init = 0.8 - 0.6 * float(np.exp(-0.3 * layer))
            qk_a = qk_prep(proj, 0, a_qk, DA_QK_DIM, (p["att_q_norm"][j], p["att_k_norm"][j]),
                           (DA_QK_DIM ** -0.5 * math.log2(math.e), 1.0))
            lamv = jnp.stack([p["att_lam_q1"][j], p["att_lam_k1"][j], p["att_lam_q2"][j],
                              p["att_lam_k2"][j]]).astype(F32)
            out_a = diff_attention(qk_a, proj, 2 * a_qk, lamv, p["att_subln"][j].astype(F32), lam_init)
            qk_b = qk_prep(proj, 3 * a_qk, a_qk, HEAD_DIM, None, (1.0, HEAD_DIM ** -0.5))
            out_b = retention(qk_b, proj, 5 * a_qk, 6 * a_qk, p["ret_norm"][j])
            w_out = p["att_w_out"][j]
            x = matmul_residual([(out_a.reshape(n, -1), w_out[:a_qk]), (out_b.reshape(n, -1), w_out[a_qk:])], x)
        else:
            proj = norm_matmul(x, p["norm_mix"][layer], p["rg_w_in"][j], out_dtype=BF16).reshape(b, s, -1)
            y = rg_lru_block(proj, p["rg_conv_w"][j], p["rg_conv_b"][j], p["rg_wa"][j], p["rg_ba"][j],
                             p["rg_wi"][j], p["rg_bi"][j], p["rg_lambda"][j])
            x = matmul_residual([(y.reshape(n, -1), p["rg_w_out"][j])], x)
        x = ec_moe(x, p["norm_ffn"][layer], p["moe_router"][layer], p["moe_w_gate"], p["moe_w_up"],
                   p["moe_w_down"], layer)
    return x.reshape(b, s, d)


def kernel(x_prompt, x_sample, norm_mix, norm_ffn, att_w_in, att_w_out, att_q_norm, att_k_norm, att_lam_q1, att_lam_k1, att_lam_q2, att_lam_k2, att_subln, ret_norm, rg_w_in, rg_conv_w, rg_conv_b, rg_wa, rg_ba, rg_wi, rg_bi, rg_lambda, rg_w_out, moe_router, moe_w_gate, moe_w_up, moe_w_down):
    p = dict(
        norm_mix=norm_mix.astype(F32), norm_ffn=norm_ffn.astype(F32),
        att_w_in=att_w_in.astype(BF16), att_w_out=att_w_out.astype(BF16),
        att_q_norm=att_q_norm, att_k_norm=att_k_norm,
        att_lam_q1=att_lam_q1, att_lam_k1=att_lam_k1, att_lam_q2=att_lam_q2, att_lam_k2=att_lam_k2,
        att_subln=att_subln, ret_norm=ret_norm,
        rg_w_in=rg_w_in.astype(BF16), rg_conv_w=rg_conv_w, rg_conv_b=rg_conv_b,
        rg_wa=rg_wa, rg_ba=rg_ba, rg_wi=rg_wi, rg_bi=rg_bi, rg_lambda=rg_lambda,
        rg_w_out=rg_w_out.astype(BF16), moe_router=moe_router,
        moe_w_gate=moe_w_gate, moe_w_up=moe_w_up, moe_w_down=moe_w_down,
    )
    return (_trunk(x_prompt, p), _trunk(x_sample, p))
```
